```python
import jax, jax.numpy as jnp
from jax import lax
import numpy as np

D_MODEL = 1024
BATCH = 8
SEQ = 4096
DEPTH = 1

GRID_W = 64
CTX_LEN = 256
D_MIX = D_MODEL
C_CONV = D_MIX // 2
CONV_K = 31
N_HEADS = 8
QK_NOPE = 64
QK_ROPE = 32
V_DIM = 64
Q_LORA = 384
KV_LORA = 256
D_FF = 2816
FFN_K = 3
Q_BLOCK = 128
ROPE_THETA = 10000.0
LN_EPS = 1e-5
ALPHA = (2 * DEPTH) ** 0.25
BETA = (8 * DEPTH) ** -0.25
D_IN = 2 * C_CONV + Q_LORA + KV_LORA + QK_ROPE
SOFTMAX_SCALE = (QK_NOPE + QK_ROPE) ** -0.5

kernel_name = 'hybrid_conformer_mla_convffn_prefix_ctx'


def layer_norm(x, g=None, b=None):
    xf = x.astype(jnp.float32)
    mu = xf.mean(-1, keepdims=True)
    var = jnp.square(xf - mu).mean(-1, keepdims=True)
    y = (xf - mu) * lax.rsqrt(var + LN_EPS)
    if g is not None:
        y = y * g.astype(jnp.float32) + b.astype(jnp.float32)
    return y.astype(x.dtype)


def rms_norm(x, g):
    xf = x.astype(jnp.float32)
    y = xf * lax.rsqrt(jnp.square(xf).mean(-1, keepdims=True) + LN_EPS) * g.astype(jnp.float32)
    return y.astype(x.dtype)


def dwconv(x, w, b):
    k = w.shape[0]
    y = lax.conv_general_dilated(x, w[:, None, :], window_strides=(1,),
                                 padding=[(k // 2, k // 2)],
                                 dimension_numbers=('NWC', 'WIO', 'NWC'),
                                 feature_group_count=x.shape[-1])
    return y + b


def axial_rope_tables(n, dtype):
    rows = n // GRID_W
    row = jnp.repeat(jnp.arange(rows), GRID_W)
    col = jnp.tile(jnp.arange(GRID_W), rows)
    n_freq = QK_ROPE // 4
    inv = ROPE_THETA ** (-jnp.arange(n_freq, dtype=jnp.float32) / n_freq)
    ang = jnp.concatenate([row[:, None] * inv, col[:, None] * inv], axis=-1)
    return jnp.cos(ang).astype(dtype), jnp.sin(ang).astype(dtype)


def apply_rope(x, cos, sin):
    half = x.shape[-1] // 2
    x1, x2 = x[..., :half], x[..., half:]
    return jnp.concatenate([x1 * cos - x2 * sin, x2 * cos + x1 * sin], axis=-1)


def mixer_inputs(h, w_in):
    z = h @ w_in
    return jnp.split(z, [2 * C_CONV, 2 * C_CONV + Q_LORA, 2 * C_CONV + Q_LORA + KV_LORA], axis=-1)


def conformer_conv(u, dw_w, dw_b, ln_g, ln_b):
    a, gte = jnp.split(u, 2, axis=-1)
    v = a * jax.nn.sigmoid(gte)
    v = dwconv(v, dw_w, dw_b)
    return jax.nn.silu(layer_norm(v, ln_g, ln_b))


def mla_queries(cq, q_norm_g, w_uq, cos, sin):
    b, n = cq.shape[:2]
    q = (rms_norm(cq, q_norm_g) @ w_uq).reshape(b, n, N_HEADS, QK_NOPE + QK_ROPE)
    q_nope, q_rope = q[..., :QK_NOPE], q[..., QK_NOPE:]
    if cos is not None:
        q_rope = apply_rope(q_rope, cos[:, None, :], sin[:, None, :])
    return q_nope, q_rope


def mla_keys_values(ckv, kr, kv_norm_g, w_ukv, cos, sin):
    b, n = ckv.shape[:2]
    kv = (rms_norm(ckv, kv_norm_g) @ w_ukv).reshape(b, n, N_HEADS, QK_NOPE + V_DIM)
    k_nope, v = kv[..., :QK_NOPE], kv[..., QK_NOPE:]
    k_rope = kr if cos is None else apply_rope(kr, cos, sin)
    return k_nope, k_rope, v


def attend(q_nope, q_rope, k_nope, k_rope, v):
    s = (jnp.einsum('bqhd,bkhd->bhqk', q_nope, k_nope)
         + jnp.einsum('bqhr,bkr->bhqk', q_rope, k_rope)).astype(jnp.float32) * SOFTMAX_SCALE
    p = jax.nn.softmax(s, axis=-1).astype(v.dtype)
    return jnp.einsum('bhqk,bkhd->bqhd', p, v)


def attend_blocked(q_nope, q_rope, k_nope, k_rope, v):
    b, n = q_nope.shape[:2]
    nb = n // Q_BLOCK
    qn = q_nope.reshape(b, nb, Q_BLOCK, N_HEADS, QK_NOPE).transpose(1, 0, 2, 3, 4)
    qr = q_rope.reshape(b, nb, Q_BLOCK, N_HEADS, QK_ROPE).transpose(1, 0, 2, 3, 4)
    out = lax.map(lambda qs: attend(qs[0], qs[1], k_nope, k_rope, v), (qn, qr))
    return out.transpose(1, 0, 2, 3, 4).reshape(b, n, N_HEADS, V_DIM)


def merge_groups(conv_out, attn_out, w_o, b_o):
    b, n = conv_out.shape[:2]
    cat = jnp.concatenate([conv_out, attn_out.reshape(b, n, N_HEADS * V_DIM)], axis=-1)
    return cat @ w_o + b_o


def conv_ffn(h, w_up, dw_w, dw_b, w_down, b_down):
    u = dwconv(h @ w_up, dw_w, dw_b)
    g, val = jnp.split(u, 2, axis=-1)
    return (jax.nn.silu(g) * val) @ w_down + b_down


def setup_inputs(seed: int = 0) -> dict:
    key = jax.random.key(seed)
    ks = jax.random.split(key, 32)
    L = DEPTH

    def nrm(k, shape, s):
        return s * jax.random.normal(k, shape, jnp.float32)

    return {
        'x': nrm(ks[0], (BATCH, SEQ, D_MODEL), 1.0),
        'c': nrm(ks[1], (BATCH, D_MODEL), 1.0),
        'ctx': nrm(ks[2], (BATCH, CTX_LEN, D_MODEL), 1.0),
        'c_ctx': nrm(ks[3], (D_MODEL,), 1.0),
        'w_ada': nrm(ks[4], (L, D_MODEL, 6 * D_MODEL), D_MODEL ** -0.5),
        'b_ada': nrm(ks[5], (L, 6 * D_MODEL), 0.02),
        'w_in': nrm(ks[6], (L, D_MODEL, D_IN), D_MODEL ** -0.5),
        'conv_dw_w': nrm(ks[7], (L, CONV_K, C_CONV), CONV_K ** -0.5),
        'conv_dw_b': nrm(ks[8], (L, C_CONV), 0.02),
        'conv_ln_g': 1.0 + nrm(ks[9], (L, C_CONV), 0.05),
        'conv_ln_b': nrm(ks[10], (L, C_CONV), 0.02),
        'q_norm_g': 1.0 + nrm(ks[11], (L, Q_LORA), 0.05),
        'w_uq': nrm(ks[12], (L, Q_LORA, N_HEADS * (QK_NOPE + QK_ROPE)), Q_LORA ** -0.5),
        'kv_norm_g': 1.0 + nrm(ks[13], (L, KV_LORA), 0.05),
        'w_ukv': nrm(ks[14], (L, KV_LORA, N_HEADS * (QK_NOPE + V_DIM)), KV_LORA ** -0.5),
        'w_o': nrm(ks[15], (L, D_MIX, D_MODEL), BETA * D_MIX ** -0.5),
        'b_o': nrm(ks[16], (L, D_MODEL), 0.02),
        'ln1_g': 1.0 + nrm(ks[17], (L, D_MODEL), 0.05),
        'ln1_b': nrm(ks[18], (L, D_MODEL), 0.02),
        'w_up': nrm(ks[19], (L, D_MODEL, 2 * D_FF), D_MODEL ** -0.5),
        'ffn_dw_w': nrm(ks[20], (L, FFN_K, 2 * D_FF), FFN_K ** -0.5),
        'ffn_dw_b': nrm(ks[21], (L, 2 * D_FF), 0.02),
        'w_down': nrm(ks[22], (L, D_FF, D_MODEL), BETA * D_FF ** -0.5),
        'b_down': nrm(ks[23], (L, D_MODEL), 0.02),
        'ln2_g': 1.0 + nrm(ks[24], (L, D_MODEL), 0.05),
        'ln2_b': nrm(ks[25], (L, D_MODEL), 0.02),
    }


def reference(x, c, ctx, c_ctx, w_ada, b_ada, w_in, conv_dw_w, conv_dw_b, conv_ln_g, conv_ln_b,
              q_norm_g, w_uq, kv_norm_g, w_ukv, w_o, b_o, ln1_g, ln1_b,
              w_up, ffn_dw_w, ffn_dw_b, w_down, b_down, ln2_g, ln2_b):
    n = x.shape[1]
    cos, sin = axial_rope_tables(n, x.dtype)
    x = layer_norm(x)
    ctx = layer_norm(ctx)
    for i in range(DEPTH):
        last = i == DEPTH - 1
        mod = jax.nn.silu(c) @ w_ada[i] + b_ada[i]
        sh1, sc1, g1, sh2, sc2, g2 = jnp.split(mod[:, None, :], 6, axis=-1)
        modc = jax.nn.silu(c_ctx) @ w_ada[i] + b_ada[i]
        sh1c, sc1c, g1c, sh2c, sc2c, g2c = jnp.split(modc, 6)

        u_c, cq_c, ckv_c, kr_c = mixer_inputs(ctx * (1 + sc1c) + sh1c, w_in[i])
        k_nope_c, k_rope_c, v_c = mla_keys_values(ckv_c, kr_c, kv_norm_g[i], w_ukv[i], None, None)

        u_l, cq_l, ckv_l, kr_l = mixer_inputs(x * (1 + sc1) + sh1, w_in[i])
        k_nope_l, k_rope_l, v_l = mla_keys_values(ckv_l, kr_l, kv_norm_g[i], w_ukv[i], cos, sin)
        q_nope_l, q_rope_l = mla_queries(cq_l, q_norm_g[i], w_uq[i], cos, sin)
        attn_l = attend_blocked(q_nope_l, q_rope_l,
                                jnp.concatenate([k_nope_c, k_nope_l], axis=1),
                                jnp.concatenate([k_rope_c, k_rope_l], axis=1),
                                jnp.concatenate([v_c, v_l], axis=1))
        conv_l = conformer_conv(u_l, conv_dw_w[i], conv_dw_b[i], conv_ln_g[i], conv_ln_b[i])
        y_l = merge_groups(conv_l, attn_l, w_o[i], b_o[i])
        x_mid = layer_norm(ALPHA * x + g1 * y_l, ln1_g[i], ln1_b[i])

        if not last:
            q_nope_c, q_rope_c = mla_queries(cq_c, q_norm_g[i], w_uq[i], None, None)
            attn_c = attend(q_nope_c, q_rope_c, k_nope_c, k_rope_c, v_c)
            conv_c = conformer_conv(u_c, conv_dw_w[i], conv_dw_b[i], conv_ln_g[i], conv_ln_b[i])
            y_c = merge_groups(conv_c, attn_c, w_o[i], b_o[i])
            ctx = layer_norm(ALPHA * ctx + g1c * y_c, ln1_g[i], ln1_b[i])
            f_c = conv_ffn(ctx * (1 + sc2c) + sh2c, w_up[i], ffn_dw_w[i], ffn_dw_b[i], w_down[i], b_down[i])
            ctx = layer_norm(ALPHA * ctx + g2c * f_c, ln2_g[i], ln2_b[i])

        f_l = conv_ffn(x_mid * (1 + sc2) + sh2, w_up[i], ffn_dw_w[i], ffn_dw_b[i], w_down[i], b_down[i])
        x = layer_norm(ALPHA * x_mid + g2 * f_l, ln2_g[i], ln2_b[i])
    return x
```

```python
import functools
import math

import jax
import jax.numpy as jnp
from jax import lax
from jax.experimental import pallas as pl
from jax.experimental.pallas import tpu as pltpu

F32 = jnp.float32
BF16 = jnp.bfloat16

D_MODEL = 1024
GRID_W = 64
C_CONV = 512
CONV_K = 31
N_HEADS = 8
QK_NOPE = 64
QK_ROPE = 32
V_DIM = 64
Q_LORA = 384
KV_LORA = 256
D_FF = 2816
FFN_K = 3
ROPE_THETA = 10000.0
LN_EPS = 1e-5
DEPTH = 1
ALPHA = (2 * DEPTH) ** 0.25
SOFTMAX_SCALE = (QK_NOPE + QK_ROPE) ** -0.5
LOG2E = math.log2(math.e)

LANES = 128
HEAD_PAIRS = N_HEADS // 2
HALO = 16
FF_CHUNK = 256
VMEM_LIMIT = 52 * 1024 * 1024

_Z_CQ = 2 * C_CONV
_Z_CKV = _Z_CQ + Q_LORA
_Z_KR = _Z_CKV + KV_LORA
_Z_KRS = _Z_KR + LANES
_Z_END = _Z_KRS + LANES


def _sigmoid(x):
    return 1.0 / (1.0 + jnp.exp(-x))


def _norm(x):
    mu = jnp.mean(x, axis=-1, keepdims=True)
    xc = x - mu
    var = jnp.mean(xc * xc, axis=-1, keepdims=True)
    return xc * lax.rsqrt(var + LN_EPS)


def _rms(x, g):
    return x * lax.rsqrt(jnp.mean(x * x, axis=-1, keepdims=True) + LN_EPS) * g


def _dot(a, b):
    return jnp.dot(a, b, preferred_element_type=F32)


def _params(*sem):
    return pltpu.CompilerParams(dimension_semantics=sem, vmem_limit_bytes=VMEM_LIMIT)


def _const_spec(shape):
    zeros = (0,) * len(shape)
    return pl.BlockSpec(shape, lambda *_: zeros)


def _mod_kernel(c_ref, w_ref, b_ref, o_ref):
    c = c_ref[...]
    a = c * _sigmoid(c)
    o_ref[...] = jnp.dot(a, w_ref[...], precision=lax.Precision.HIGHEST,
                         preferred_element_type=F32) + b_ref[...]


def _mod_call(c_rows, w_ada, b_ada):
    rows, d = c_rows.shape
    n = w_ada.shape[1]
    bn = 1536
    return pl.pallas_call(
        _mod_kernel,
        grid=(n // bn,),
        in_specs=[pl.BlockSpec((rows, d), lambda j: (0, 0)),
                  pl.BlockSpec((d, bn), lambda j: (0, j)),
                  pl.BlockSpec((1, bn), lambda j: (0, j))],
        out_specs=pl.BlockSpec((rows, bn), lambda j: (0, j)),
        out_shape=jax.ShapeDtypeStruct((rows, n), F32),
        compiler_params=_params("arbitrary"),
        name="mod",
    )(c_rows, w_ada, b_ada)


def _store_kv(ckv, kr128, gkv_ref, wk_ref, wv_ref, k_ref, v_ref):
    ckvn = _rms(ckv, gkv_ref[...]).astype(BF16)
    kk = _dot(ckvn, wk_ref[...])
    for h in range(N_HEADS):
        k_ref[0, h] = (kk[:, h * LANES:(h + 1) * LANES] + kr128).astype(BF16)
    vv = _dot(ckvn, wv_ref[...])
    for p in range(HEAD_PAIRS):
        v_ref[0, p] = vv[:, p * LANES:(p + 1) * LANES].astype(BF16)


def _proj_kernel(x_ref, sc_ref, sh_ref, cos_ref, sin_ref, win_ref, gq_ref, wqa_ref, wqb_ref,
                 gkv_ref, wk_ref, wv_ref, glu_ref, q_ref, k_ref, v_ref):
    h = _norm(x_ref[0]) * (1.0 + sc_ref[0]) + sh_ref[0]
    z = _dot(h.astype(BF16), win_ref[...])
    glu_ref[0] = z[:, :C_CONV] * _sigmoid(z[:, C_CONV:_Z_CQ])
    cos = cos_ref[...]
    sin = sin_ref[...]
    cqn = _rms(z[:, _Z_CQ:_Z_CKV], gq_ref[...]).astype(BF16)
    qa = _dot(cqn, wqa_ref[...])
    qb = _dot(cqn, wqb_ref[...])
    for hd in range(N_HEADS):
        sl = slice(hd * LANES, (hd + 1) * LANES)
        q = (qa[:, sl] * cos + qb[:, sl] * sin) * (SOFTMAX_SCALE * LOG2E)
        q_ref[0, hd] = q.astype(BF16)
    kr128 = z[:, _Z_KR:_Z_KRS] * cos + z[:, _Z_KRS:_Z_END] * sin
    _store_kv(z[:, _Z_CKV:_Z_KR], kr128, gkv_ref, wk_ref, wv_ref, k_ref, v_ref)


def _proj_ctx_kernel(x_ref, sc_ref, sh_ref, win_ref, gkv_ref, wk_ref, wv_ref, kin_ref, vin_ref,
                     k_ref, v_ref):
    del kin_ref, vin_ref
    h = _norm(x_ref[0]) * (1.0 + sc_ref[0]) + sh_ref[0]
    z = _dot(h.astype(BF16), win_ref[...])
    _store_kv(z[:, :KV_LORA], z[:, KV_LORA:], gkv_ref, wk_ref, wv_ref, k_ref, v_ref)


def _proj_call(x, sc, sh, cos_t, sin_t, w, n_keys, tm):
    b, n, d = x.shape
    nt = n // tm
    tok = lambda bi, i: (bi, i, 0)
    row = lambda bi, i: (bi, 0, 0)
    tab = lambda bi, i: (i, 0)
    hd4 = lambda bi, i: (bi, 0, i, 0)
    return pl.pallas_call(
        _proj_kernel,
        grid=(b, nt),
        in_specs=[pl.BlockSpec((1, tm, d), tok),
                  pl.BlockSpec((1, 1, d), row), pl.BlockSpec((1, 1, d), row),
                  pl.BlockSpec((tm, LANES), tab), pl.BlockSpec((tm, LANES), tab),
                  _const_spec(w["win"].shape), _const_spec(w["gq"].shape),
                  _const_spec(w["wqa"].shape), _const_spec(w["wqb"].shape),
                  _const_spec(w["gkv"].shape), _const_spec(w["wk"].shape), _const_spec(w["wv"].shape)],
        out_specs=[pl.BlockSpec((1, tm, C_CONV), tok),
                   pl.BlockSpec((1, N_HEADS, tm, LANES), hd4),
                   pl.BlockSpec((1, N_HEADS, tm, LANES), hd4),
                   pl.BlockSpec((1, HEAD_PAIRS, tm, LANES), hd4)],
        out_shape=[jax.ShapeDtypeStruct((b, n, C_CONV), F32),
                   jax.ShapeDtypeStruct((b, N_HEADS, n, LANES), BF16),
                   jax.ShapeDtypeStruct((b, N_HEADS, n_keys, LANES), BF16),
                   jax.ShapeDtypeStruct((b, HEAD_PAIRS, n_keys, LANES), BF16)],
        compiler_params=_params("parallel", "parallel"),
        name="proj",
    )(x, sc, sh, cos_t, sin_t, w["win"], w["gq"], w["wqa"], w["wqb"], w["gkv"], w["wk"], w["wv"])


def _proj_ctx_call(ctx, sc, sh, w, k_all, v_all, key_block):
    b, n_ctx, d = ctx.shape
    row = lambda bi: (0, 0, 0)
    kv = lambda bi: (bi, 0, key_block, 0)
    return pl.pallas_call(
        _proj_ctx_kernel,
        grid=(b,),
        in_specs=[pl.BlockSpec((1, n_ctx, d), lambda bi: (bi, 0, 0)),
                  pl.BlockSpec((1, 1, d), row), pl.BlockSpec((1, 1, d), row),
                  _const_spec(w["win_ctx"].shape), _const_spec(w["gkv"].shape),
                  _const_spec(w["wk"].shape), _const_spec(w["wv"].shape),
                  pl.BlockSpec(memory_space=pl.ANY), pl.BlockSpec(memory_space=pl.ANY)],
        out_specs=[pl.BlockSpec((1, N_HEADS, n_ctx, LANES), kv),
                   pl.BlockSpec((1, HEAD_PAIRS, n_ctx, LANES), kv)],
        out_shape=[jax.ShapeDtypeStruct(k_all.shape, BF16), jax.ShapeDtypeStruct(v_all.shape, BF16)],
        input_output_aliases={7: 0, 8: 1},
        compiler_params=_params("parallel"),
        name="proj_ctx",
    )(ctx, sc, sh, w["win_ctx"], w["gkv"], w["wk"], w["wv"], k_all, v_all)


def _attn_kernel(q_ref, k_ref, v_ref, o_ref):
    v = v_ref[0, 0]
    outs = []
    for j in range(2):
        s = lax.dot_general(q_ref[0, j], k_ref[0, j], (((1,), (1,)), ((), ())),
                            preferred_element_type=F32)
        m = jnp.max(s, axis=-1, keepdims=True)
        p = jnp.exp2(s - m)
        l = jnp.sum(p, axis=-1, keepdims=True)
        outs.append(_dot(p.astype(BF16), v) / l)
    lane = lax.broadcasted_iota(jnp.int32, outs[0].shape, 1)
    o_ref[0] = jnp.where(lane < V_DIM, outs[0], outs[1]).astype(BF16)


def _attn_call(q, k, v, tq):
    b, _, n, _ = q.shape
    nk = k.shape[2]
    return pl.pallas_call(
        _attn_kernel,
        grid=(b, HEAD_PAIRS, n // tq),
        in_specs=[pl.BlockSpec((1, 2, tq, LANES), lambda bi, p, i: (bi, p, i, 0)),
                  pl.BlockSpec((1, 2, nk, LANES), lambda bi, p, i: (bi, p, 0, 0)),
                  pl.BlockSpec((1, 1, nk, LANES), lambda bi, p, i: (bi, p, 0, 0))],
        out_specs=pl.BlockSpec((1, tq, LANES), lambda bi, p, i: (bi, i, p)),
        out_shape=jax.ShapeDtypeStruct((b, n, HEAD_PAIRS * LANES), BF16),
        compiler_params=_params("parallel", "parallel", "arbitrary"),
        name="attn",
    )(q, k, v)


def _halo_specs(tm, n, width):
    per = tm // HALO
    last = n // HALO - 1
    return (pl.BlockSpec((1, HALO, width), lambda bi, i: (bi, jnp.maximum(i * per - 1, 0), 0)),
            pl.BlockSpec((1, tm, width), lambda bi, i: (bi, i, 0)),
            pl.BlockSpec((1, HALO, width), lambda bi, i: (bi, jnp.minimum((i + 1) * per, last), 0)))


def _halo_masks():
    i = pl.program_id(1)
    has_prev = (i > 0).astype(F32)
    has_next = (i < pl.num_programs(1) - 1).astype(F32)
    return has_prev, has_next


CONV_ROWS = 32


def _mix_kernel(gp_ref, g_ref, gn_ref, a_ref, x_ref, g1_ref, dww_ref, dwb_ref, cg_ref, cb_ref,
                wo_ref, bo_ref, lg_ref, lb_ref, o_ref, win_ref, conv_ref):
    tm = g_ref.shape[1]
    has_prev, has_next = _halo_masks()
    win_ref[0:HALO] = gp_ref[0] * has_prev
    win_ref[HALO:HALO + tm] = g_ref[0]
    win_ref[HALO + tm:] = gn_ref[0] * has_next
    half = CONV_K // 2

    def rows(c, carry):
        base = pl.multiple_of(c * CONV_ROWS, CONV_ROWS)
        wv = win_ref[pl.ds(base, CONV_ROWS + 2 * HALO), :]
        acc = jnp.zeros((CONV_ROWS, C_CONV), F32) + dwb_ref[...]
        for kk in range(CONV_K):
            off = HALO - half + kk
            acc = acc + wv[off:off + CONV_ROWS] * dww_ref[kk:kk + 1, :]
        y = _norm(acc) * cg_ref[...] + cb_ref[...]
        conv_ref[pl.ds(base, CONV_ROWS), :] = (y * _sigmoid(y)).astype(BF16)
        return carry

    lax.fori_loop(0, tm // CONV_ROWS, rows, 0)
    y = (_dot(conv_ref[...], wo_ref[:C_CONV, :]) + _dot(a_ref[0], wo_ref[C_CONV:, :]) + bo_ref[...])
    o_ref[0] = _norm(ALPHA * _norm(x_ref[0]) + g1_ref[0] * y) * lg_ref[...] + lb_ref[...]


def _mix_call(glu, attn, x, g1, w, tm):
    b, n, d = x.shape
    tok = lambda bi, i: (bi, i, 0)
    row = lambda bi, i: (bi, 0, 0)
    return pl.pallas_call(
        _mix_kernel,
        grid=(b, n // tm),
        in_specs=[*_halo_specs(tm, n, C_CONV),
                  pl.BlockSpec((1, tm, HEAD_PAIRS * LANES), tok),
                  pl.BlockSpec((1, tm, d), tok),
                  pl.BlockSpec((1, 1, d), row),
                  _const_spec(w["dww"].shape), _const_spec(w["dwb"].shape),
                  _const_spec(w["cg"].shape), _const_spec(w["cb"].shape),
                  _const_spec(w["wo"].shape), _const_spec(w["bo"].shape),
                  _const_spec(w["l1g"].shape), _const_spec(w["l1b"].shape)],
        out_specs=pl.BlockSpec((1, tm, d), tok),
        out_shape=jax.ShapeDtypeStruct((b, n, d), F32),
        scratch_shapes=[pltpu.VMEM((tm + 2 * HALO, C_CONV), F32), pltpu.VMEM((tm, C_CONV), BF16)],
        compiler_params=_params("parallel", "parallel"),
        name="mix",
    )(glu, glu, glu, attn, x, g1, w["dww"], w["dwb"], w["cg"], w["cb"], w["wo"], w["bo"],
      w["l1g"], w["l1b"])


FFN_ROWS = 64


def _ffn_kernel(xp_ref, x_ref, xn_ref, sc_ref, sh_ref, g2_ref, wup_ref, fw_ref, fb_ref, wdn_ref,
                bdn_ref, lg_ref, lb_ref, o_ref, h_ref, ug_ref, uv_ref, act_ref):
    tm = x_ref.shape[1]
    has_prev, has_next = _halo_masks()
    scale = 1.0 + sc_ref[0]
    shift = sh_ref[0]
    h_ref[0:HALO] = ((xp_ref[0] * scale + shift) * has_prev).astype(BF16)
    h_ref[HALO:HALO + tm] = (x_ref[0] * scale + shift).astype(BF16)
    h_ref[HALO + tm:] = ((xn_ref[0] * scale + shift) * has_next).astype(BF16)

    for c in range(D_FF // FF_CHUNK):
        cg = slice(c * FF_CHUNK, (c + 1) * FF_CHUNK)
        cv = slice(D_FF + c * FF_CHUNK, D_FF + (c + 1) * FF_CHUNK)
        ug_ref[...] = _dot(h_ref[...], wup_ref[:, cg])
        uv_ref[...] = _dot(h_ref[...], wup_ref[:, cv])

        def rows(r, carry):
            base = pl.multiple_of(r * FFN_ROWS, FFN_ROWS)

            def conv(u_ref, cols):
                uw = u_ref[pl.ds(base, FFN_ROWS + 2 * HALO), :]
                acc = fb_ref[:, cols]
                for kk in range(FFN_K):
                    off = HALO - 1 + kk
                    acc = acc + uw[off:off + FFN_ROWS] * fw_ref[kk:kk + 1, cols]
                return acc

            g = conv(ug_ref, cg)
            v = conv(uv_ref, cv)
            act_ref[pl.ds(base, FFN_ROWS), cg] = (g * _sigmoid(g) * v).astype(BF16)
            return carry

        lax.fori_loop(0, tm // FFN_ROWS, rows, 0)

    f = _dot(act_ref[...], wdn_ref[...]) + bdn_ref[...]
    o_ref[0] = _norm(ALPHA * x_ref[0] + g2_ref[0] * f) * lg_ref[...] + lb_ref[...]


def _ffn_call(xm, sc, sh, g2, w, tm):
    b, n, d = xm.shape
    tok = lambda bi, i: (bi, i, 0)
    row = lambda bi, i: (bi, 0, 0)
    return pl.pallas_call(
        _ffn_kernel,
        grid=(b, n // tm),
        in_specs=[*_halo_specs(tm, n, d),
                  pl.BlockSpec((1, 1, d), row), pl.BlockSpec((1, 1, d), row), pl.BlockSpec((1, 1, d), row),
                  _const_spec(w["wup"].shape), _const_spec(w["fw"].shape), _const_spec(w["fb"].shape),
                  _const_spec(w["wdn"].shape), _const_spec(w["bdn"].shape),
                  _const_spec(w["l2g"].shape), _const_spec(w["l2b"].shape)],
        out_specs=pl.BlockSpec((1, tm, d), tok),
        out_shape=jax.ShapeDtypeStruct((b, n, d), F32),
        scratch_shapes=[pltpu.VMEM((tm + 2 * HALO, d), BF16),
                        pltpu.VMEM((tm + 2 * HALO, FF_CHUNK), F32),
                        pltpu.VMEM((tm + 2 * HALO, FF_CHUNK), F32),
                        pltpu.VMEM((tm, D_FF), BF16)],
        compiler_params=_params("parallel", "parallel"),
        name="ffn",
    )(xm, xm, xm, sc, sh, g2, w["wup"], w["fw"], w["fb"], w["wdn"], w["bdn"], w["l2g"], w["l2b"])


def _rope_tables(n):
    t = jnp.arange(n)
    n_freq = QK_ROPE // 4
    inv = ROPE_THETA ** (-jnp.arange(n_freq, dtype=F32) / n_freq)
    ang = jnp.concatenate([(t // GRID_W)[:, None] * inv, (t % GRID_W)[:, None] * inv], axis=-1)
    cos, sin = jnp.cos(ang), jnp.sin(ang)
    pad = jnp.zeros((n, LANES - QK_NOPE - QK_ROPE), F32)
    cos_t = jnp.concatenate([jnp.ones((n, QK_NOPE), F32), cos, cos, pad], axis=-1)
    sin_t = jnp.concatenate([jnp.zeros((n, QK_NOPE), F32), sin, sin, pad], axis=-1)
    return cos_t, sin_t


def _partner(w_rope):
    half = w_rope.shape[-1] // 2
    return jnp.concatenate([-w_rope[..., half:], w_rope[..., :half]], axis=-1)


def _layout_weights(w_in, w_uq, w_ukv):
    d = w_in.shape[0]
    z = lambda r, c: jnp.zeros((r, c), F32)
    kr = w_in[:, _Z_KR:_Z_KR + QK_ROPE]
    tail = LANES - QK_NOPE - QK_ROPE
    kr_grp = jnp.concatenate([z(d, QK_NOPE), kr, z(d, tail)], axis=-1)
    krs_grp = jnp.concatenate([z(d, QK_NOPE), _partner(kr), z(d, tail)], axis=-1)
    win = jnp.concatenate([w_in[:, :_Z_KR], kr_grp, krs_grp], axis=-1)
    win_ctx = jnp.concatenate([w_in[:, _Z_CKV:_Z_KR], kr_grp], axis=-1)

    wq = w_uq.reshape(Q_LORA, N_HEADS, QK_NOPE + QK_ROPE)
    zq = jnp.zeros((Q_LORA, N_HEADS, tail), F32)
    wqa = jnp.concatenate([wq, zq], axis=-1).reshape(Q_LORA, N_HEADS * LANES)
    wqb = jnp.concatenate([jnp.zeros((Q_LORA, N_HEADS, QK_NOPE), F32), _partner(wq[..., QK_NOPE:]), zq],
                          axis=-1).reshape(Q_LORA, N_HEADS * LANES)
    wkv = w_ukv.reshape(KV_LORA, N_HEADS, QK_NOPE + V_DIM)
    wk = jnp.concatenate([wkv[..., :QK_NOPE], jnp.zeros((KV_LORA, N_HEADS, LANES - QK_NOPE), F32)],
                         axis=-1).reshape(KV_LORA, N_HEADS * LANES)
    wv = wkv[..., QK_NOPE:].reshape(KV_LORA, N_HEADS * V_DIM)
    cast = lambda a: a.astype(BF16)
    return dict(win=cast(win), win_ctx=cast(win_ctx), wqa=cast(wqa), wqb=cast(wqb), wk=cast(wk), wv=cast(wv))


def kernel(x, c, ctx, c_ctx, w_ada, b_ada, w_in, conv_dw_w, conv_dw_b, conv_ln_g, conv_ln_b, q_norm_g,
           w_uq, kv_norm_g, w_ukv, w_o, b_o, ln1_g, ln1_b, w_up, ffn_dw_w, ffn_dw_b, w_down, b_down,
           ln2_g, ln2_b):
    b, n, d = x.shape
    n_ctx = ctx.shape[1]
    n_keys = n + n_ctx
    tm = 512
    tq = 256
    assert DEPTH == 1 and w_ada.shape[0] == 1
    assert n % tm == 0 and n % tq == 0 and n % n_ctx == 0 and tm % HALO == 0

    c_rows = jnp.concatenate([c, c_ctx[None, :], jnp.zeros((16 - b - 1, d), F32)], axis=0)
    mod = _mod_call(c_rows, w_ada[0], b_ada).reshape(16, 6, 1, d)
    sh1, sc1, g1, sh2, sc2, g2 = (mod[:b, j] for j in range(6))
    sh1c, sc1c = mod[b:b + 1, 0], mod[b:b + 1, 1]

    row2 = lambda a: a.reshape(1, -1)
    w = _layout_weights(w_in[0], w_uq[0], w_ukv[0])
    w.update(gq=row2(q_norm_g[0]), gkv=row2(kv_norm_g[0]))
    cos_t, sin_t = _rope_tables(n)

    glu, q, k_all, v_all = _proj_call(x, sc1, sh1, cos_t, sin_t, w, n_keys, tm)
    k_all, v_all = _proj_ctx_call(ctx, sc1c, sh1c, w, k_all, v_all, n // n_ctx)
    attn = _attn_call(q, k_all, v_all, tq)

    wm = dict(dww=conv_dw_w[0], dwb=row2(conv_dw_b[0]), cg=row2(conv_ln_g[0]), cb=row2(conv_ln_b[0]),
              wo=w_o[0].astype(BF16), bo=row2(b_o[0]), l1g=row2(ln1_g[0]), l1b=row2(ln1_b[0]))
    x_mid = _mix_call(glu, attn, x, g1, wm, tm)

    wf = dict(wup=w_up[0].astype(BF16), fw=ffn_dw_w[0], fb=row2(ffn_dw_b[0]), wdn=w_down[0].astype(BF16),
              bdn=row2(b_down[0]), l2g=row2(ln2_g[0]), l2b=row2(ln2_b[0]))
    return _ffn_call(x_mid, sc2, sh2, g2, wf, tm)
```

```python
import functools
import math

import jax
import jax.numpy as jnp
from jax import lax
from jax.experimental import pallas as pl
from jax.experimental.pallas import tpu as pltpu

F32 = jnp.float32
BF16 = jnp.bfloat16

D_MODEL = 1024
GRID_W = 64
C_CONV = 512
CONV_K = 31
N_HEADS = 8
QK_NOPE = 64
QK_ROPE = 32
V_DIM = 64
Q_LORA = 384
KV_LORA = 256
D_FF = 2816
FFN_K = 3
ROPE_THETA = 10000.0
LN_EPS = 1e-5
DEPTH = 1
ALPHA = (2 * DEPTH) ** 0.25
SOFTMAX_SCALE = (QK_NOPE + QK_ROPE) ** -0.5
LOG2E = math.log2(math.e)

LANES = 128
SUBLANES = 8
HEAD_PAIRS = N_HEADS // 2
V_TILE = 2 * LANES
HALO = 16
FF_CHUNK = 256
KEY_CHUNK = 256
VMEM_LIMIT = 52 * 1024 * 1024

_Z_CQ = 2 * C_CONV
_Z_CKV = _Z_CQ + Q_LORA
_Z_KR = _Z_CKV + KV_LORA
_Z_KRS = _Z_KR + LANES
_Z_END = _Z_KRS + LANES


def _sigmoid(x):
    return 1.0 / (1.0 + jnp.exp(-x))


def _norm(x):
    mu = jnp.mean(x, axis=-1, keepdims=True)
    xc = x - mu
    var = jnp.mean(xc * xc, axis=-1, keepdims=True)
    return xc * lax.rsqrt(var + LN_EPS)


def _rms(x, g):
    return x * lax.rsqrt(jnp.mean(x * x, axis=-1, keepdims=True) + LN_EPS) * g


def _dot(a, b):
    return jnp.dot(a, b, preferred_element_type=F32)


def _params(*sem):
    return pltpu.CompilerParams(dimension_semantics=sem, vmem_limit_bytes=VMEM_LIMIT)


def _const_spec(shape):
    zeros = (0,) * len(shape)
    return pl.BlockSpec(shape, lambda *_: zeros)


def _mod_kernel(c_ref, w_ref, b_ref, o_ref):
    c = c_ref[...]
    a = c * _sigmoid(c)
    o_ref[...] = jnp.dot(a, w_ref[...], precision=lax.Precision.HIGHEST,
                         preferred_element_type=F32) + b_ref[...]


def _mod_call(c_rows, w_ada, b_ada):
    rows, d = c_rows.shape
    n = w_ada.shape[1]
    bn = 1536
    return pl.pallas_call(
        _mod_kernel,
        grid=(n // bn,),
        in_specs=[pl.BlockSpec((rows, d), lambda j: (0, 0)),
                  pl.BlockSpec((d, bn), lambda j: (0, j)),
                  pl.BlockSpec((1, bn), lambda j: (0, j))],
        out_specs=pl.BlockSpec((rows, bn), lambda j: (0, j)),
        out_shape=jax.ShapeDtypeStruct((rows, n), F32),
        compiler_params=_params("arbitrary"),
        name="mod",
    )(c_rows, w_ada, b_ada)


def _store_kv(ckv, kr128, gkv_ref, wk_ref, wv_ref, k_ref, v_ref):
    ckvn = _rms(ckv, gkv_ref[...]).astype(BF16)
    kk = _dot(ckvn, wk_ref[...])
    for h in range(N_HEADS):
        k_ref[0, h] = (kk[:, h * LANES:(h + 1) * LANES] + kr128).astype(BF16)
    vv = _dot(ckvn, wv_ref[...])
    one_col = (lax.broadcasted_iota(jnp.int32, (vv.shape[0], LANES), 1) == 0).astype(BF16)
    for p in range(HEAD_PAIRS):
        v_ref[0, p, :, :LANES] = vv[:, p * LANES:(p + 1) * LANES].astype(BF16)
        v_ref[0, p, :, LANES:] = one_col


def _proj_kernel(x_ref, sc_ref, sh_ref, cos_ref, sin_ref, win_ref, gq_ref, wqa_ref, wqb_ref,
                 gkv_ref, wk_ref, wv_ref, glu_ref, q_ref, k_ref, v_ref):
    h = _norm(x_ref[0]) * (1.0 + sc_ref[0]) + sh_ref[0]
    z = _dot(h.astype(BF16), win_ref[...])
    glu_ref[0] = z[:, :C_CONV] * _sigmoid(z[:, C_CONV:_Z_CQ])
    cos = cos_ref[...]
    sin = sin_ref[...]
    cqn = _rms(z[:, _Z_CQ:_Z_CKV], gq_ref[...]).astype(BF16)
    qa = _dot(cqn, wqa_ref[...])
    qb = _dot(cqn, wqb_ref[...])
    for hd in range(N_HEADS):
        sl = slice(hd * LANES, (hd + 1) * LANES)
        q = (qa[:, sl] * cos + qb[:, sl] * sin) * (SOFTMAX_SCALE * LOG2E)
        q_ref[0, hd] = q.astype(BF16)
    kr128 = z[:, _Z_KR:_Z_KRS] * cos + z[:, _Z_KRS:_Z_END] * sin
    _store_kv(z[:, _Z_CKV:_Z_KR], kr128, gkv_ref, wk_ref, wv_ref, k_ref, v_ref)


def _proj_ctx_kernel(x_ref, sc_ref, sh_ref, win_ref, gkv_ref, wk_ref, wv_ref, kin_ref, vin_ref,
                     k_ref, v_ref):
    del kin_ref, vin_ref
    h = _norm(x_ref[0]) * (1.0 + sc_ref[0]) + sh_ref[0]
    z = _dot(h.astype(BF16), win_ref[...])
    _store_kv(z[:, :KV_LORA], z[:, KV_LORA:], gkv_ref, wk_ref, wv_ref, k_ref, v_ref)


def _proj_call(x, sc, sh, cos_t, sin_t, w, n_keys, tm):
    b, n, d = x.shape
    nt = n // tm
    tok = lambda bi, i: (bi, i, 0)
    row = lambda bi, i: (bi, 0, 0)
    tab = lambda bi, i: (i, 0)
    hd4 = lambda bi, i: (bi, 0, i, 0)
    return pl.pallas_call(
        _proj_kernel,
        grid=(b, nt),
        in_specs=[pl.BlockSpec((1, tm, d), tok),
                  pl.BlockSpec((1, 1, d), row), pl.BlockSpec((1, 1, d), row),
                  pl.BlockSpec((tm, LANES), tab), pl.BlockSpec((tm, LANES), tab),
                  _const_spec(w["win"].shape), _const_spec(w["gq"].shape),
                  _const_spec(w["wqa"].shape), _const_spec(w["wqb"].shape),
                  _const_spec(w["gkv"].shape), _const_spec(w["wk"].shape), _const_spec(w["wv"].shape)],
        out_specs=[pl.BlockSpec((1, tm, C_CONV), tok),
                   pl.BlockSpec((1, N_HEADS, tm, LANES), hd4),
                   pl.BlockSpec((1, N_HEADS, tm, LANES), hd4),
                   pl.BlockSpec((1, HEAD_PAIRS, tm, V_TILE), hd4)],
        out_shape=[jax.ShapeDtypeStruct((b, n, C_CONV), F32),
                   jax.ShapeDtypeStruct((b, N_HEADS, n, LANES), BF16),
                   jax.ShapeDtypeStruct((b, N_HEADS, n_keys, LANES), BF16),
                   jax.ShapeDtypeStruct((b, HEAD_PAIRS, n_keys, V_TILE), BF16)],
        compiler_params=_params("parallel", "parallel"),
        name="proj",
    )(x, sc, sh, cos_t, sin_t, w["win"], w["gq"], w["wqa"], w["wqb"], w["gkv"], w["wk"], w["wv"])


def _proj_ctx_call(ctx, sc, sh, w, k_all, v_all, key_block):
    b, n_ctx, d = ctx.shape
    row = lambda bi: (0, 0, 0)
    kv = lambda bi: (bi, 0, key_block, 0)
    return pl.pallas_call(
        _proj_ctx_kernel,
        grid=(b,),
        in_specs=[pl.BlockSpec((1, n_ctx, d), lambda bi: (bi, 0, 0)),
                  pl.BlockSpec((1, 1, d), row), pl.BlockSpec((1, 1, d), row),
                  _const_spec(w["win_ctx"].shape), _const_spec(w["gkv"].shape),
                  _const_spec(w["wk"].shape), _const_spec(w["wv"].shape),
                  pl.BlockSpec(memory_space=pl.ANY), pl.BlockSpec(memory_space=pl.ANY)],
        out_specs=[pl.BlockSpec((1, N_HEADS, n_ctx, LANES), kv),
                   pl.BlockSpec((1, HEAD_PAIRS, n_ctx, V_TILE), kv)],
        out_shape=[jax.ShapeDtypeStruct(k_all.shape, BF16), jax.ShapeDtypeStruct(v_all.shape, BF16)],
        input_output_aliases={7: 0, 8: 1},
        compiler_params=_params("parallel"),
        name="proj_ctx",
    )(ctx, sc, sh, w["win_ctx"], w["gkv"], w["wk"], w["wv"], k_all, v_all)


def _attn_kernel(q_ref, k_ref, v_ref, o_ref):
    nk = k_ref.shape[2]
    n_chunks = nk // KEY_CHUNK

    def scores(j, c):
        return lax.dot_general(q_ref[0, j], k_ref[0, j, c * KEY_CHUNK:(c + 1) * KEY_CHUNK],
                               (((1,), (1,)), ((), ())), preferred_element_type=F32)

    def row_max(s):
        m = s[0]
        for sc in s[1:]:
            m = jnp.maximum(m, sc)
        return jnp.max(m, axis=-1, keepdims=True)

    def weighted_values(s, m, c, acc):
        pv = _dot(jnp.exp2(s[c] - m).astype(BF16), v_ref[0, 0, c * KEY_CHUNK:(c + 1) * KEY_CHUNK])
        return pv if acc is None else acc + pv

    s_prev = [scores(0, c) for c in range(n_chunks)]
    m_prev = row_max(s_prev)
    s_next, acc = [], None
    for c in range(n_chunks):
        s_next.append(scores(1, c))
        acc = weighted_values(s_prev, m_prev, c, acc)
    accs = [acc]
    m_next, acc = row_max(s_next), None
    for c in range(n_chunks):
        acc = weighted_values(s_next, m_next, c, acc)
    accs.append(acc)
    outs = [a[:, :LANES] * (1.0 / a[:, LANES:LANES + 1]) for a in accs]
    lane = lax.broadcasted_iota(jnp.int32, outs[0].shape, 1)
    o_ref[0] = jnp.where(lane < V_DIM, outs[0], outs[1]).astype(BF16)


def _attn_call(q, k, v, tq):
    b, _, n, _ = q.shape
    nk = k.shape[2]
    return pl.pallas_call(
        _attn_kernel,
        grid=(b, HEAD_PAIRS, n // tq),
        in_specs=[pl.BlockSpec((1, 2, tq, LANES), lambda bi, p, i: (bi, p, i, 0)),
                  pl.BlockSpec((1, 2, nk, LANES), lambda bi, p, i: (bi, p, 0, 0)),
                  pl.BlockSpec((1, 1, nk, V_TILE), lambda bi, p, i: (bi, p, 0, 0))],
        out_specs=pl.BlockSpec((1, tq, LANES), lambda bi, p, i: (bi, i, p)),
        out_shape=jax.ShapeDtypeStruct((b, n, HEAD_PAIRS * LANES), BF16),
        compiler_params=_params("parallel", "parallel", "arbitrary"),
        name="attn",
    )(q, k, v)


def _halo_specs(tm, n, width):
    per = tm // HALO
    last = n // HALO - 1
    return (pl.BlockSpec((1, HALO, width), lambda bi, i: (bi, jnp.maximum(i * per - 1, 0), 0)),
            pl.BlockSpec((1, tm, width), lambda bi, i: (bi, i, 0)),
            pl.BlockSpec((1, HALO, width), lambda bi, i: (bi, jnp.minimum((i + 1) * per, last), 0)))


def _halo_masks():
    i = pl.program_id(1)
    has_prev = (i > 0).astype(F32)
    has_next = (i < pl.num_programs(1) - 1).astype(F32)
    return has_prev, has_next


CONV_ROWS = 32


def _mix_kernel(gp_ref, g_ref, gn_ref, a_ref, x_ref, g1_ref, dww_ref, dwb_ref, cg_ref, cb_ref,
                wo_ref, bo_ref, lg_ref, lb_ref, o_ref, sh_ref, conv_ref):
    tm = g_ref.shape[1]
    has_prev, has_next = _halo_masks()
    sh_ref[0, 0:HALO] = gp_ref[0] * has_prev
    sh_ref[0, HALO:HALO + tm] = g_ref[0]
    sh_ref[0, HALO + tm:2 * HALO + tm] = gn_ref[0] * has_next
    sh_ref[0, 2 * HALO + tm:] = jnp.zeros((SUBLANES, C_CONV), F32)

    def shift(c, carry):
        base = pl.multiple_of(c * CONV_ROWS, CONV_ROWS)
        wv = sh_ref[0, pl.ds(base, CONV_ROWS + SUBLANES), :]
        for r in range(1, SUBLANES):
            sh_ref[r, pl.ds(base, CONV_ROWS), :] = wv[r:r + CONV_ROWS]
        return carry

    lax.fori_loop(0, (tm + 2 * HALO) // CONV_ROWS, shift, 0)
    half = CONV_K // 2

    def rows(c, carry):
        base = pl.multiple_of(c * CONV_ROWS, CONV_ROWS)
        acc = jnp.zeros((CONV_ROWS, C_CONV), F32) + dwb_ref[...]
        for kk in range(CONV_K):
            off = HALO - half + kk
            r = off % SUBLANES
            start = pl.multiple_of(base + (off - r), SUBLANES)
            acc = acc + sh_ref[r, pl.ds(start, CONV_ROWS), :] * dww_ref[kk:kk + 1, :]
        conv_ref[pl.ds(base, CONV_ROWS), :] = acc
        return carry

    lax.fori_loop(0, tm // CONV_ROWS, rows, 0)
    c = _norm(conv_ref[...]) * cg_ref[...] + cb_ref[...]
    c = (c * _sigmoid(c)).astype(BF16)
    y = _dot(c, wo_ref[:C_CONV, :]) + _dot(a_ref[0], wo_ref[C_CONV:, :]) + bo_ref[...]
    o_ref[0] = _norm(ALPHA * _norm(x_ref[0]) + g1_ref[0] * y) * lg_ref[...] + lb_ref[...]


def _mix_call(glu, attn, x, g1, w, tm):
    b, n, d = x.shape
    tok = lambda bi, i: (bi, i, 0)
    row = lambda bi, i: (bi, 0, 0)
    return pl.pallas_call(
        _mix_kernel,
        grid=(b, n // tm),
        in_specs=[*_halo_specs(tm, n, C_CONV),
                  pl.BlockSpec((1, tm, HEAD_PAIRS * LANES), tok),
                  pl.BlockSpec((1, tm, d), tok),
                  pl.BlockSpec((1, 1, d), row),
                  _const_spec(w["dww"].shape), _const_spec(w["dwb"].shape),
                  _const_spec(w["cg"].shape), _const_spec(w["cb"].shape),
                  _const_spec(w["wo"].shape), _const_spec(w["bo"].shape),
                  _const_spec(w["l1g"].shape), _const_spec(w["l1b"].shape)],
        out_specs=pl.BlockSpec((1, tm, d), tok),
        out_shape=jax.ShapeDtypeStruct((b, n, d), F32),
        scratch_shapes=[pltpu.VMEM((SUBLANES, tm + 2 * HALO + SUBLANES, C_CONV), F32),
                        pltpu.VMEM((tm, C_CONV), F32)],
        compiler_params=_params("parallel", "parallel"),
        name="mix",
    )(glu, glu, glu, attn, x, g1, w["dww"], w["dwb"], w["cg"], w["cb"], w["wo"], w["bo"],
      w["l1g"], w["l1b"])


def _ffn_kernel(xp_ref, x_ref, xn_ref, sc_ref, sh_ref, g2_ref, wup_ref, fw_ref, fb_ref, wdn_ref,
                bdn_ref, lg_ref, lb_ref, o_ref, h_ref, act_ref):
    tm = x_ref.shape[1]
    has_prev, has_next = _halo_masks()
    scale = 1.0 + sc_ref[0]
    shift = sh_ref[0]
    h_ref[0:HALO] = ((xp_ref[0] * scale + shift) * has_prev).astype(BF16)
    h_ref[HALO:HALO + tm] = (x_ref[0] * scale + shift).astype(BF16)
    h_ref[HALO + tm:] = ((xn_ref[0] * scale + shift) * has_next).astype(BF16)

    def conv(u, cols):
        acc = fb_ref[:, cols]
        for kk in range(FFN_K):
            off = HALO - 1 + kk
            acc = acc + u[off:off + tm] * fw_ref[kk:kk + 1, cols]
        return acc

    for c in range(D_FF // FF_CHUNK):
        cg = slice(c * FF_CHUNK, (c + 1) * FF_CHUNK)
        cv = slice(D_FF + c * FF_CHUNK, D_FF + (c + 1) * FF_CHUNK)
        g = conv(_dot(h_ref[...], wup_ref[:, cg]), cg)
        v = conv(_dot(h_ref[...], wup_ref[:, cv]), cv)
        act_ref[:, cg] = (g * _sigmoid(g) * v).astype(BF16)

    f = _dot(act_ref[...], wdn_ref[...]) + bdn_ref[...]
    o_ref[0] = _norm(ALPHA * x_ref[0] + g2_ref[0] * f) * lg_ref[...] + lb_ref[...]


def _ffn_call(xm, sc, sh, g2, w, tm):
    b, n, d = xm.shape
    tok = lambda bi, i: (bi, i, 0)
    row = lambda bi, i: (bi, 0, 0)
    return pl.pallas_call(
        _ffn_kernel,
        grid=(b, n // tm),
        in_specs=[*_halo_specs(tm, n, d),
                  pl.BlockSpec((1, 1, d), row), pl.BlockSpec((1, 1, d), row), pl.BlockSpec((1, 1, d), row),
                  _const_spec(w["wup"].shape), _const_spec(w["fw"].shape), _const_spec(w["fb"].shape),
                  _const_spec(w["wdn"].shape), _const_spec(w["bdn"].shape),
                  _const_spec(w["l2g"].shape), _const_spec(w["l2b"].shape)],
        out_specs=pl.BlockSpec((1, tm, d), tok),
        out_shape=jax.ShapeDtypeStruct((b, n, d), F32),
        scratch_shapes=[pltpu.VMEM((tm + 2 * HALO, d), BF16), pltpu.VMEM((tm, D_FF), BF16)],
        compiler_params=_params("parallel", "parallel"),
        name="ffn",
    )(xm, xm, xm, sc, sh, g2, w["wup"], w["fw"], w["fb"], w["wdn"], w["bdn"], w["l2g"], w["l2b"])


def _rope_tables(n):
    t = jnp.arange(n)
    n_freq = QK_ROPE // 4
    inv = ROPE_THETA ** (-jnp.arange(n_freq, dtype=F32) / n_freq)
    ang = jnp.concatenate([(t // GRID_W)[:, None] * inv, (t % GRID_W)[:, None] * inv], axis=-1)
    cos, sin = jnp.cos(ang), jnp.sin(ang)
    pad = jnp.zeros((n, LANES - QK_NOPE - QK_ROPE), F32)
    cos_t = jnp.concatenate([jnp.ones((n, QK_NOPE), F32), cos, cos, pad], axis=-1)
    sin_t = jnp.concatenate([jnp.zeros((n, QK_NOPE), F32), sin, sin, pad], axis=-1)
    return cos_t, sin_t


def _partner(w_rope):
    half = w_rope.shape[-1] // 2
    return jnp.concatenate([-w_rope[..., half:], w_rope[..., :half]], axis=-1)


def _layout_weights(w_in, w_uq, w_ukv):
    d = w_in.shape[0]
    z = lambda r, c: jnp.zeros((r, c), F32)
    kr = w_in[:, _Z_KR:_Z_KR + QK_ROPE]
    tail = LANES - QK_NOPE - QK_ROPE
    kr_grp = jnp.concatenate([z(d, QK_NOPE), kr, z(d, tail)], axis=-1)
    krs_grp = jnp.concatenate([z(d, QK_NOPE), _partner(kr), z(d, tail)], axis=-1)
    win = jnp.concatenate([w_in[:, :_Z_KR], kr_grp, krs_grp], axis=-1)
    win_ctx = jnp.concatenate([w_in[:, _Z_CKV:_Z_KR], kr_grp], axis=-1)

    wq = w_uq.reshape(Q_LORA, N_HEADS, QK_NOPE + QK_ROPE)
    zq = jnp.zeros((Q_LORA, N_HEADS, tail), F32)
    wqa = jnp.concatenate([wq, zq], axis=-1).reshape(Q_LORA, N_HEADS * LANES)
    wqb = jnp.concatenate([jnp.zeros((Q_LORA, N_HEADS, QK_NOPE), F32), _partner(wq[..., QK_NOPE:]), zq],
                          axis=-1).reshape(Q_LORA, N_HEADS * LANES)
    wkv = w_ukv.reshape(KV_LORA, N_HEADS, QK_NOPE + V_DIM)
    wk = jnp.concatenate([wkv[..., :QK_NOPE], jnp.zeros((KV_LORA, N_HEADS, LANES - QK_NOPE), F32)],
                         axis=-1).reshape(KV_LORA, N_HEADS * LANES)
    wv = wkv[..., QK_NOPE:].reshape(KV_LORA, N_HEADS * V_DIM)
    cast = lambda a: a.astype(BF16)
    return dict(win=cast(win), win_ctx=cast(win_ctx), wqa=cast(wqa), wqb=cast(wqb), wk=cast(wk), wv=cast(wv))


def kernel(x, c, ctx, c_ctx, w_ada, b_ada, w_in, conv_dw_w, conv_dw_b, conv_ln_g, conv_ln_b, q_norm_g,
           w_uq, kv_norm_g, w_ukv, w_o, b_o, ln1_g, ln1_b, w_up, ffn_dw_w, ffn_dw_b, w_down, b_down,
           ln2_g, ln2_b):
    b, n, d = x.shape
    n_ctx = ctx.shape[1]
    n_keys = n + n_ctx
    tm = 512
    tq = 512
    assert DEPTH == 1 and w_ada.shape[0] == 1
    assert n % tm == 0 and n % tq == 0 and n % n_ctx == 0 and tm % HALO == 0

    c_rows = jnp.concatenate([c, c_ctx[None, :], jnp.zeros((16 - b - 1, d), F32)], axis=0)
    mod = _mod_call(c_rows, w_ada[0], b_ada).reshape(16, 6, 1, d)
    sh1, sc1, g1, sh2, sc2, g2 = (mod[:b, j] for j in range(6))
    sh1c, sc1c = mod[b:b + 1, 0], mod[b:b + 1, 1]

    row2 = lambda a: a.reshape(1, -1)
    w = _layout_weights(w_in[0], w_uq[0], w_ukv[0])
    w.update(gq=row2(q_norm_g[0]), gkv=row2(kv_norm_g[0]))
    cos_t, sin_t = _rope_tables(n)

    glu, q, k_all, v_all = _proj_call(x, sc1, sh1, cos_t, sin_t, w, n_keys, tm)
    k_all, v_all = _proj_ctx_call(ctx, sc1c, sh1c, w, k_all, v_all, n // n_ctx)
    attn = _attn_call(q, k_all, v_all, tq)

    wm = dict(dww=conv_dw_w[0], dwb=row2(conv_dw_b[0]), cg=row2(conv_ln_g[0]), cb=row2(conv_ln_b[0]),
              wo=w_o[0].astype(BF16), bo=row2(b_o[0]), l1g=row2(ln1_g[0]), l1b=row2(ln1_b[0]))
    x_mid = _mix_call(glu, attn, x, g1, wm, tm)

    wf = dict(wup=w_up[0].astype(BF16), fw=ffn_dw_w[0], fb=row2(ffn_dw_b[0]), wdn=w_down[0].astype(BF16),
              bdn=row2(b_down[0]), l2g=row2(ln2_g[0]), l2b=row2(ln2_b[0]))
    return _ffn_call(x_mid, sc2, sh2, g2, wf, tm)
```

```python
import functools
import math

import jax
import jax.numpy as jnp
from jax import lax
from jax.experimental import pallas as pl
from jax.experimental.pallas import tpu as pltpu

F32 = jnp.float32
BF16 = jnp.bfloat16

D_MODEL = 1024
GRID_W = 64
C_CONV = 512
CONV_K = 31
N_HEADS = 8
QK_NOPE = 64
QK_ROPE = 32
V_DIM = 64
Q_LORA = 384
KV_LORA = 256
D_FF = 2816
FFN_K = 3
ROPE_THETA = 10000.0
LN_EPS = 1e-5
DEPTH = 1
ALPHA = (2 * DEPTH) ** 0.25
SOFTMAX_SCALE = (QK_NOPE + QK_ROPE) ** -0.5
LOG2E = math.log2(math.e)

LANES = 128
SUBLANES = 8
HEAD_PAIRS = N_HEADS // 2
V_TILE = 2 * LANES
HALO = 16
FF_CHUNK = 256
KEY_CHUNK = 256
VMEM_LIMIT = 52 * 1024 * 1024

_Z_CQ = 2 * C_CONV
_Z_CKV = _Z_CQ + Q_LORA
_Z_KR = _Z_CKV + KV_LORA
_Z_END = _Z_KR + LANES
PARTNER_ROLL = LANES - QK_ROPE


def _sigmoid(x):
    return 0.5 * jnp.tanh(0.5 * x) + 0.5


def _silu(x):
    h = 0.5 * x
    return h * jnp.tanh(h) + h


def _norm(x):
    mu = jnp.mean(x, axis=-1, keepdims=True)
    xc = x - mu
    var = jnp.mean(xc * xc, axis=-1, keepdims=True)
    return xc * lax.rsqrt(var + LN_EPS)


def _rms(x, g):
    return x * lax.rsqrt(jnp.mean(x * x, axis=-1, keepdims=True) + LN_EPS) * g


def _dot(a, b):
    return jnp.dot(a, b, preferred_element_type=F32)


def _params(*sem):
    return pltpu.CompilerParams(dimension_semantics=sem, vmem_limit_bytes=VMEM_LIMIT)


def _const_spec(shape):
    zeros = (0,) * len(shape)
    return pl.BlockSpec(shape, lambda *_: zeros)


def _mod_kernel(c_ref, w_ref, b_ref, o_ref):
    c = c_ref[...]
    a = _silu(c)
    o_ref[...] = jnp.dot(a, w_ref[...], precision=lax.Precision.HIGHEST,
                         preferred_element_type=F32) + b_ref[...]


def _mod_call(c_rows, w_ada, b_ada):
    rows, d = c_rows.shape
    n = w_ada.shape[1]
    bn = 1536
    return pl.pallas_call(
        _mod_kernel,
        grid=(n // bn,),
        in_specs=[pl.BlockSpec((rows, d), lambda j: (0, 0)),
                  pl.BlockSpec((d, bn), lambda j: (0, j)),
                  pl.BlockSpec((1, bn), lambda j: (0, j))],
        out_specs=pl.BlockSpec((rows, bn), lambda j: (0, j)),
        out_shape=jax.ShapeDtypeStruct((rows, n), F32),
        compiler_params=_params("arbitrary"),
        name="mod",
    )(c_rows, w_ada, b_ada)


def _store_kv(ckv, kr128, gkv_ref, wk_ref, wv_ref, k_ref, v_ref):
    ckvn = _rms(ckv, gkv_ref[...]).astype(BF16)
    kk = _dot(ckvn, wk_ref[...])
    for h in range(N_HEADS):
        k_ref[0, h] = (kk[:, h * LANES:(h + 1) * LANES] + kr128).astype(BF16)
    vv = _dot(ckvn, wv_ref[...])
    one_col = (lax.broadcasted_iota(jnp.int32, (vv.shape[0], LANES), 1) == 0).astype(BF16)
    for p in range(HEAD_PAIRS):
        v_ref[0, p, :, :LANES] = vv[:, p * LANES:(p + 1) * LANES].astype(BF16)
        v_ref[0, p, :, LANES:] = one_col


def _rotary(t, cos, sin):
    return t * cos + pltpu.roll(t, PARTNER_ROLL, 1) * sin


def _proj_kernel(x_ref, sc_ref, sh_ref, cos_ref, sin_ref, win_ref, gq_ref, wq_ref,
                 gkv_ref, wk_ref, wv_ref, glu_ref, q_ref, k_ref, v_ref):
    h = _norm(x_ref[0]) * (1.0 + sc_ref[0]) + sh_ref[0]
    z = _dot(h.astype(BF16), win_ref[...])
    glu_ref[0] = z[:, :C_CONV] * _sigmoid(z[:, C_CONV:_Z_CQ])
    cos = cos_ref[...]
    sin = sin_ref[...]
    cqn = _rms(z[:, _Z_CQ:_Z_CKV], gq_ref[...]).astype(BF16)
    qa = _dot(cqn, wq_ref[...])
    for hd in range(N_HEADS):
        q = _rotary(qa[:, hd * LANES:(hd + 1) * LANES], cos, sin) * (SOFTMAX_SCALE * LOG2E)
        q_ref[0, hd] = q.astype(BF16)
    kr128 = _rotary(z[:, _Z_KR:_Z_END], cos, sin)
    _store_kv(z[:, _Z_CKV:_Z_KR], kr128, gkv_ref, wk_ref, wv_ref, k_ref, v_ref)


def _proj_ctx_kernel(x_ref, sc_ref, sh_ref, win_ref, gkv_ref, wk_ref, wv_ref, kin_ref, vin_ref,
                     k_ref, v_ref):
    del kin_ref, vin_ref
    h = _norm(x_ref[0]) * (1.0 + sc_ref[0]) + sh_ref[0]
    z = _dot(h.astype(BF16), win_ref[...])
    _store_kv(z[:, :KV_LORA], z[:, KV_LORA:], gkv_ref, wk_ref, wv_ref, k_ref, v_ref)


def _proj_call(x, sc, sh, cos_t, sin_t, w, n_keys, tm):
    b, n, d = x.shape
    nt = n // tm
    tok = lambda bi, i: (bi, i, 0)
    row = lambda bi, i: (bi, 0, 0)
    tab = lambda bi, i: (i, 0)
    hd4 = lambda bi, i: (bi, 0, i, 0)
    return pl.pallas_call(
        _proj_kernel,
        grid=(b, nt),
        in_specs=[pl.BlockSpec((1, tm, d), tok),
                  pl.BlockSpec((1, 1, d), row), pl.BlockSpec((1, 1, d), row),
                  pl.BlockSpec((tm, LANES), tab), pl.BlockSpec((tm, LANES), tab),
                  _const_spec(w["win"].shape), _const_spec(w["gq"].shape), _const_spec(w["wq"].shape),
                  _const_spec(w["gkv"].shape), _const_spec(w["wk"].shape), _const_spec(w["wv"].shape)],
        out_specs=[pl.BlockSpec((1, tm, C_CONV), tok),
                   pl.BlockSpec((1, N_HEADS, tm, LANES), hd4),
                   pl.BlockSpec((1, N_HEADS, tm, LANES), hd4),
                   pl.BlockSpec((1, HEAD_PAIRS, tm, V_TILE), hd4)],
        out_shape=[jax.ShapeDtypeStruct((b, n, C_CONV), F32),
                   jax.ShapeDtypeStruct((b, N_HEADS, n, LANES), BF16),
                   jax.ShapeDtypeStruct((b, N_HEADS, n_keys, LANES), BF16),
                   jax.ShapeDtypeStruct((b, HEAD_PAIRS, n_keys, V_TILE), BF16)],
        compiler_params=_params("parallel", "parallel"),
        name="proj",
    )(x, sc, sh, cos_t, sin_t, w["win"], w["gq"], w["wq"], w["gkv"], w["wk"], w["wv"])


def _proj_ctx_call(ctx, sc, sh, w, k_all, v_all, key_block):
    b, n_ctx, d = ctx.shape
    row = lambda bi: (0, 0, 0)
    kv = lambda bi: (bi, 0, key_block, 0)
    return pl.pallas_call(
        _proj_ctx_kernel,
        grid=(b,),
        in_specs=[pl.BlockSpec((1, n_ctx, d), lambda bi: (bi, 0, 0)),
                  pl.BlockSpec((1, 1, d), row), pl.BlockSpec((1, 1, d), row),
                  _const_spec(w["win_ctx"].shape), _const_spec(w["gkv"].shape),
                  _const_spec(w["wk"].shape), _const_spec(w["wv"].shape),
                  pl.BlockSpec(memory_space=pl.ANY), pl.BlockSpec(memory_space=pl.ANY)],
        out_specs=[pl.BlockSpec((1, N_HEADS, n_ctx, LANES), kv),
                   pl.BlockSpec((1, HEAD_PAIRS, n_ctx, V_TILE), kv)],
        out_shape=[jax.ShapeDtypeStruct(k_all.shape, BF16), jax.ShapeDtypeStruct(v_all.shape, BF16)],
        input_output_aliases={7: 0, 8: 1},
        compiler_params=_params("parallel"),
        name="proj_ctx",
    )(ctx, sc, sh, w["win_ctx"], w["gkv"], w["wk"], w["wv"], k_all, v_all)


def _attn_kernel(q_ref, k_ref, v_ref, o_ref):
    nk = k_ref.shape[2]
    n_chunks = nk // KEY_CHUNK

    def scores(j, c):
        return lax.dot_general(q_ref[0, j], k_ref[0, j, c * KEY_CHUNK:(c + 1) * KEY_CHUNK],
                               (((1,), (1,)), ((), ())), preferred_element_type=F32)

    def row_max(s):
        m = s[0]
        for sc in s[1:]:
            m = jnp.maximum(m, sc)
        return jnp.max(m, axis=-1, keepdims=True)

    def weighted_values(s, m, c, acc):
        pv = _dot(jnp.exp2(s[c] - m).astype(BF16), v_ref[0, 0, c * KEY_CHUNK:(c + 1) * KEY_CHUNK])
        return pv if acc is None else acc + pv

    s_prev = [scores(0, c) for c in range(n_chunks)]
    m_prev = row_max(s_prev)
    s_next, acc = [], None
    for c in range(n_chunks):
        s_next.append(scores(1, c))
        acc = weighted_values(s_prev, m_prev, c, acc)
    accs = [acc]
    m_next, acc = row_max(s_next), None
    for c in range(n_chunks):
        acc = weighted_values(s_next, m_next, c, acc)
    accs.append(acc)
    outs = [a[:, :LANES] * (1.0 / a[:, LANES:LANES + 1]) for a in accs]
    lane = lax.broadcasted_iota(jnp.int32, outs[0].shape, 1)
    o_ref[0] = jnp.where(lane < V_DIM, outs[0], outs[1]).astype(BF16)


def _attn_call(q, k, v, tq):
    b, _, n, _ = q.shape
    nk = k.shape[2]
    return pl.pallas_call(
        _attn_kernel,
        grid=(b, HEAD_PAIRS, n // tq),
        in_specs=[pl.BlockSpec((1, 2, tq, LANES), lambda bi, p, i: (bi, p, i, 0)),
                  pl.BlockSpec((1, 2, nk, LANES), lambda bi, p, i: (bi, p, 0, 0)),
                  pl.BlockSpec((1, 1, nk, V_TILE), lambda bi, p, i: (bi, p, 0, 0))],
        out_specs=pl.BlockSpec((1, tq, LANES), lambda bi, p, i: (bi, i, p)),
        out_shape=jax.ShapeDtypeStruct((b, n, HEAD_PAIRS * LANES), BF16),
        compiler_params=_params("parallel", "parallel", "arbitrary"),
        name="attn",
    )(q, k, v)


def _halo_specs(tm, n, width):
    per = tm // HALO
    last = n // HALO - 1
    return (pl.BlockSpec((1, HALO, width), lambda bi, i: (bi, jnp.maximum(i * per - 1, 0), 0)),
            pl.BlockSpec((1, tm, width), lambda bi, i: (bi, i, 0)),
            pl.BlockSpec((1, HALO, width), lambda bi, i: (bi, jnp.minimum((i + 1) * per, last), 0)))


def _halo_masks():
    i = pl.program_id(1)
    has_prev = (i > 0).astype(F32)
    has_next = (i < pl.num_programs(1) - 1).astype(F32)
    return has_prev, has_next


CONV_ROWS = 32


def _mix_kernel(gp_ref, g_ref, gn_ref, a_ref, x_ref, g1_ref, dww_ref, dwb_ref, cg_ref, cb_ref,
                wo_ref, bo_ref, lg_ref, lb_ref, o_ref, win_ref, conv_ref):
    tm = g_ref.shape[1]
    has_prev, has_next = _halo_masks()
    for j in range(C_CONV // LANES):
        cols = slice(j * LANES, (j + 1) * LANES)
        win_ref[j, 0:HALO] = gp_ref[0, :, cols] * has_prev
        win_ref[j, HALO:HALO + tm] = g_ref[0, :, cols]
        win_ref[j, HALO + tm:] = gn_ref[0, :, cols] * has_next
    half = CONV_K // 2

    def rows(c, carry):
        base = pl.multiple_of(c * CONV_ROWS, CONV_ROWS)
        for j in range(C_CONV // LANES):
            cols = slice(j * LANES, (j + 1) * LANES)
            acc = jnp.zeros((CONV_ROWS, LANES), F32) + dwb_ref[:, cols]
            for kk in range(CONV_K):
                acc = acc + (win_ref[j, pl.ds(base + (HALO - half + kk), CONV_ROWS), :]
                             * dww_ref[kk:kk + 1, cols])
            conv_ref[pl.ds(base, CONV_ROWS), cols] = acc
        return carry

    lax.fori_loop(0, tm // CONV_ROWS, rows, 0)
    c = _silu(_norm(conv_ref[...]) * cg_ref[...] + cb_ref[...]).astype(BF16)
    y = _dot(c, wo_ref[:C_CONV, :]) + _dot(a_ref[0], wo_ref[C_CONV:, :]) + bo_ref[...]
    o_ref[0] = _norm(ALPHA * _norm(x_ref[0]) + g1_ref[0] * y) * lg_ref[...] + lb_ref[...]


def _mix_call(glu, attn, x, g1, w, tm):
    b, n, d = x.shape
    tok = lambda bi, i: (bi, i, 0)
    row = lambda bi, i: (bi, 0, 0)
    return pl.pallas_call(
        _mix_kernel,
        grid=(b, n // tm),
        in_specs=[*_halo_specs(tm, n, C_CONV),
                  pl.BlockSpec((1, tm, HEAD_PAIRS * LANES), tok),
                  pl.BlockSpec((1, tm, d), tok),
                  pl.BlockSpec((1, 1, d), row),
                  _const_spec(w["dww"].shape), _const_spec(w["dwb"].shape),
                  _const_spec(w["cg"].shape), _const_spec(w["cb"].shape),
                  _const_spec(w["wo"].shape), _const_spec(w["bo"].shape),
                  _const_spec(w["l1g"].shape), _const_spec(w["l1b"].shape)],
        out_specs=pl.BlockSpec((1, tm, d), tok),
        out_shape=jax.ShapeDtypeStruct((b, n, d), F32),
        scratch_shapes=[pltpu.VMEM((C_CONV // LANES, tm + 2 * HALO, LANES), F32),
                        pltpu.VMEM((tm, C_CONV), F32)],
        compiler_params=_params("parallel", "parallel"),
        name="mix",
    )(glu, glu, glu, attn, x, g1, w["dww"], w["dwb"], w["cg"], w["cb"], w["wo"], w["bo"],
      w["l1g"], w["l1b"])


def _ffn_kernel(xp_ref, x_ref, xn_ref, sc_ref, sh_ref, g2_ref, wup_ref, fw_ref, fb_ref, wdn_ref,
                bdn_ref, lg_ref, lb_ref, o_ref, h_ref, u_ref, act_ref):
    tm = x_ref.shape[1]
    has_prev, has_next = _halo_masks()
    scale = 1.0 + sc_ref[0]
    shift = sh_ref[0]
    h_ref[0:HALO] = ((xp_ref[0] * scale + shift) * has_prev).astype(BF16)
    h_ref[HALO:HALO + tm] = (x_ref[0] * scale + shift).astype(BF16)
    h_ref[HALO + tm:] = ((xn_ref[0] * scale + shift) * has_next).astype(BF16)

    groups = FF_CHUNK // LANES

    def conv(slot, col0, u):
        outs = []
        for j in range(groups):
            u_ref[slot + j] = u[:, j * LANES:(j + 1) * LANES]
            cols = slice(col0 + j * LANES, col0 + (j + 1) * LANES)
            acc = fb_ref[:, cols]
            for kk in range(FFN_K):
                acc = acc + u_ref[slot + j, pl.ds(HALO - 1 + kk, tm), :] * fw_ref[kk:kk + 1, cols]
            outs.append(acc)
        return outs

    for c in range(D_FF // FF_CHUNK):
        g = conv(4 * (c % 2), c * FF_CHUNK, _dot(h_ref[...], wup_ref[:, c * FF_CHUNK:(c + 1) * FF_CHUNK]))
        v = conv(4 * (c % 2) + 2, D_FF + c * FF_CHUNK,
                 _dot(h_ref[...], wup_ref[:, D_FF + c * FF_CHUNK:D_FF + (c + 1) * FF_CHUNK]))
        for j in range(groups):
            cols = slice(c * FF_CHUNK + j * LANES, c * FF_CHUNK + (j + 1) * LANES)
            act_ref[:, cols] = (_silu(g[j]) * v[j]).astype(BF16)

    f = _dot(act_ref[...], wdn_ref[...]) + bdn_ref[...]
    o_ref[0] = _norm(ALPHA * x_ref[0] + g2_ref[0] * f) * lg_ref[...] + lb_ref[...]


def _ffn_call(xm, sc, sh, g2, w, tm):
    b, n, d = xm.shape
    tok = lambda bi, i: (bi, i, 0)
    row = lambda bi, i: (bi, 0, 0)
    return pl.pallas_call(
        _ffn_kernel,
        grid=(b, n // tm),
        in_specs=[*_halo_specs(tm, n, d),
                  pl.BlockSpec((1, 1, d), row), pl.BlockSpec((1, 1, d), row), pl.BlockSpec((1, 1, d), row),
                  _const_spec(w["wup"].shape), _const_spec(w["fw"].shape), _const_spec(w["fb"].shape),
                  _const_spec(w["wdn"].shape), _const_spec(w["bdn"].shape),
                  _const_spec(w["l2g"].shape), _const_spec(w["l2b"].shape)],
        out_specs=pl.BlockSpec((1, tm, d), tok),
        out_shape=jax.ShapeDtypeStruct((b, n, d), F32),
        scratch_shapes=[pltpu.VMEM((tm + 2 * HALO, d), BF16),
                        pltpu.VMEM((4 * FF_CHUNK // LANES, tm + 2 * HALO, LANES), F32),
                        pltpu.VMEM((tm, D_FF), BF16)],
        compiler_params=_params("parallel", "parallel"),
        name="ffn",
    )(xm, xm, xm, sc, sh, g2, w["wup"], w["fw"], w["fb"], w["wdn"], w["bdn"], w["l2g"], w["l2b"])


def _rope_tables(n):
    t = jnp.arange(n)
    n_freq = QK_ROPE // 4
    inv = ROPE_THETA ** (-jnp.arange(n_freq, dtype=F32) / n_freq)
    ang = jnp.concatenate([(t // GRID_W)[:, None] * inv, (t % GRID_W)[:, None] * inv], axis=-1)
    cos, sin = jnp.cos(ang), jnp.sin(ang)
    pad = jnp.zeros((n, LANES - QK_NOPE - QK_ROPE), F32)
    cos_t = jnp.concatenate([jnp.ones((n, QK_NOPE), F32), cos, cos, pad], axis=-1)
    sin_t = jnp.concatenate([jnp.zeros((n, QK_NOPE), F32), sin, sin, pad], axis=-1)
    return cos_t, sin_t


def _partner(w_rope):
    half = w_rope.shape[-1] // 2
    return jnp.concatenate([-w_rope[..., half:], w_rope[..., :half]], axis=-1)


def _layout_weights(w_in, w_uq, w_ukv):
    d = w_in.shape[0]
    z = lambda r, c: jnp.zeros((r, c), F32)
    kr = w_in[:, _Z_KR:_Z_KR + QK_ROPE]
    tail = LANES - QK_NOPE - QK_ROPE
    win = jnp.concatenate([w_in[:, :_Z_KR], z(d, QK_NOPE), kr, _partner(kr)], axis=-1)
    win_ctx = jnp.concatenate([w_in[:, _Z_CKV:_Z_KR], z(d, QK_NOPE), kr, z(d, tail)], axis=-1)

    wq = w_uq.reshape(Q_LORA, N_HEADS, QK_NOPE + QK_ROPE)
    wq = jnp.concatenate([wq, _partner(wq[..., QK_NOPE:])], axis=-1).reshape(Q_LORA, N_HEADS * LANES)
    wkv = w_ukv.reshape(KV_LORA, N_HEADS, QK_NOPE + V_DIM)
    wk = jnp.concatenate([wkv[..., :QK_NOPE], jnp.zeros((KV_LORA, N_HEADS, LANES - QK_NOPE), F32)],
                         axis=-1).reshape(KV_LORA, N_HEADS * LANES)
    wv = wkv[..., QK_NOPE:].reshape(KV_LORA, N_HEADS * V_DIM)
    cast = lambda a: a.astype(BF16)
    return dict(win=cast(win), win_ctx=cast(win_ctx), wq=cast(wq), wk=cast(wk), wv=cast(wv))


def kernel(x, c, ctx, c_ctx, w_ada, b_ada, w_in, conv_dw_w, conv_dw_b, conv_ln_g, conv_ln_b, q_norm_g,
           w_uq, kv_norm_g, w_ukv, w_o, b_o, ln1_g, ln1_b, w_up, ffn_dw_w, ffn_dw_b, w_down, b_down,
           ln2_g, ln2_b):
    b, n, d = x.shape
    n_ctx = ctx.shape[1]
    n_keys = n + n_ctx
    tm = 512
    tq = 512
    assert DEPTH == 1 and w_ada.shape[0] == 1
    assert n % tm == 0 and n % tq == 0 and n % n_ctx == 0 and tm % HALO == 0

    c_rows = jnp.concatenate([c, c_ctx[None, :], jnp.zeros((16 - b - 1, d), F32)], axis=0)
    mod = _mod_call(c_rows, w_ada[0], b_ada).reshape(16, 6, 1, d)
    sh1, sc1, g1, sh2, sc2, g2 = (mod[:b, j] for j in range(6))
    sh1c, sc1c = mod[b:b + 1, 0], mod[b:b + 1, 1]

    row2 = lambda a: a.reshape(1, -1)
    w = _layout_weights(w_in[0], w_uq[0], w_ukv[0])
    w.update(gq=row2(q_norm_g[0]), gkv=row2(kv_norm_g[0]))
    cos_t, sin_t = _rope_tables(n)

    glu, q, k_all, v_all = _proj_call(x, sc1, sh1, cos_t, sin_t, w, n_keys, tm)
    k_all, v_all = _proj_ctx_call(ctx, sc1c, sh1c, w, k_all, v_all, n // n_ctx)
    attn = _attn_call(q, k_all, v_all, tq)

    wm = dict(dww=conv_dw_w[0], dwb=row2(conv_dw_b[0]), cg=row2(conv_ln_g[0]), cb=row2(conv_ln_b[0]),
              wo=w_o[0].astype(BF16), bo=row2(b_o[0]), l1g=row2(ln1_g[0]), l1b=row2(ln1_b[0]))
    x_mid = _mix_call(glu, attn, x, g1, wm, tm)

    wf = dict(wup=w_up[0].astype(BF16), fw=ffn_dw_w[0], fb=row2(ffn_dw_b[0]), wdn=w_down[0].astype(BF16),
              bdn=row2(b_down[0]), l2g=row2(ln2_g[0]), l2b=row2(ln2_b[0]))
    return _ffn_call(x_mid, sc2, sh2, g2, wf, tm)
```

```python
import functools
import math

import jax
import jax.numpy as jnp
from jax import lax
from jax.experimental import pallas as pl
from jax.experimental.pallas import tpu as pltpu

F32 = jnp.float32
BF16 = jnp.bfloat16

D_MODEL = 1024
GRID_W = 64
C_CONV = 512
CONV_K = 31
N_HEADS = 8
QK_NOPE = 64
QK_ROPE = 32
V_DIM = 64
Q_LORA = 384
KV_LORA = 256
D_FF = 2816
FFN_K = 3
ROPE_THETA = 10000.0
LN_EPS = 1e-5
DEPTH = 1
ALPHA = (2 * DEPTH) ** 0.25
SOFTMAX_SCALE = (QK_NOPE + QK_ROPE) ** -0.5
LOG2E = math.log2(math.e)

LANES = 128
SUBLANES = 8
HEAD_PAIRS = N_HEADS // 2
V_TILE = 2 * LANES
HALO = 16
FF_CHUNK = 256
KEY_CHUNK = 256
ATTN_ROWS = 512
PROJ_ROWS = 512
VMEM_LIMIT = 52 * 1024 * 1024

_Z_CQ = 2 * C_CONV
_Z_CKV = _Z_CQ + Q_LORA
_Z_KR = _Z_CKV + KV_LORA
_Z_END = _Z_KR + LANES
PARTNER_ROLL = LANES - QK_ROPE


def _sigmoid(x):
    return 0.5 * jnp.tanh(0.5 * x) + 0.5


def _silu(x):
    h = 0.5 * x
    return h * jnp.tanh(h) + h


def _norm(x):
    mu = jnp.mean(x, axis=-1, keepdims=True)
    xc = x - mu
    var = jnp.mean(xc * xc, axis=-1, keepdims=True)
    return xc * lax.rsqrt(var + LN_EPS)


def _rms(x, g):
    return x * lax.rsqrt(jnp.mean(x * x, axis=-1, keepdims=True) + LN_EPS) * g


def _dot(a, b):
    return jnp.dot(a, b, preferred_element_type=F32)


def _params(*sem):
    return pltpu.CompilerParams(dimension_semantics=sem, vmem_limit_bytes=VMEM_LIMIT)


def _const_spec(shape):
    zeros = (0,) * len(shape)
    return pl.BlockSpec(shape, lambda *_: zeros)


def _mod_kernel(c_ref, w_ref, b_ref, o_ref):
    c = c_ref[...]
    a = _silu(c)
    o_ref[...] = jnp.dot(a, w_ref[...], precision=lax.Precision.HIGHEST,
                         preferred_element_type=F32) + b_ref[...]


def _mod_call(c_rows, w_ada, b_ada):
    rows, d = c_rows.shape
    n = w_ada.shape[1]
    bn = 1536
    return pl.pallas_call(
        _mod_kernel,
        grid=(n // bn,),
        in_specs=[pl.BlockSpec((rows, d), lambda j: (0, 0)),
                  pl.BlockSpec((d, bn), lambda j: (0, j)),
                  pl.BlockSpec((1, bn), lambda j: (0, j))],
        out_specs=pl.BlockSpec((rows, bn), lambda j: (0, j)),
        out_shape=jax.ShapeDtypeStruct((rows, n), F32),
        compiler_params=_params("arbitrary"),
        name="mod",
    )(c_rows, w_ada, b_ada)


def _store_kv(ckv, kr128, gkv_ref, wk_ref, wv_ref, k_ref, v_ref, rows):
    ckvn = _rms(ckv, gkv_ref[...]).astype(BF16)
    kk = _dot(ckvn, wk_ref[...])
    for h in range(N_HEADS):
        k_ref[0, h, rows] = (kk[:, h * LANES:(h + 1) * LANES] + kr128).astype(BF16)
    vv = _dot(ckvn, wv_ref[...])
    one_col = (lax.broadcasted_iota(jnp.int32, (vv.shape[0], LANES), 1) == 0).astype(BF16)
    for p in range(HEAD_PAIRS):
        v_ref[0, p, rows, :LANES] = vv[:, p * LANES:(p + 1) * LANES].astype(BF16)
        v_ref[0, p, rows, LANES:] = one_col


def _rotary(t, cos, sin):
    return t * cos + pltpu.roll(t, PARTNER_ROLL, 1) * sin


def _proj_kernel(x_ref, sc_ref, sh_ref, cos_ref, sin_ref, win_ref, gq_ref, wq_ref,
                 gkv_ref, wk_ref, wv_ref, glu_ref, q_ref, k_ref, v_ref):
    for r0 in range(0, x_ref.shape[1], PROJ_ROWS):
        rows = slice(r0, r0 + PROJ_ROWS)
        h = _norm(x_ref[0, rows]) * (1.0 + sc_ref[0]) + sh_ref[0]
        z = _dot(h.astype(BF16), win_ref[...])
        glu_ref[0, rows] = z[:, :C_CONV] * _sigmoid(z[:, C_CONV:_Z_CQ])
        cos = cos_ref[rows]
        sin = sin_ref[rows]
        cqn = _rms(z[:, _Z_CQ:_Z_CKV], gq_ref[...]).astype(BF16)
        qa = _dot(cqn, wq_ref[...])
        for hd in range(N_HEADS):
            q = _rotary(qa[:, hd * LANES:(hd + 1) * LANES], cos, sin) * (SOFTMAX_SCALE * LOG2E)
            q_ref[0, hd, rows] = q.astype(BF16)
        kr128 = _rotary(z[:, _Z_KR:_Z_END], cos, sin)
        _store_kv(z[:, _Z_CKV:_Z_KR], kr128, gkv_ref, wk_ref, wv_ref, k_ref, v_ref, rows)


def _proj_ctx_kernel(x_ref, sc_ref, sh_ref, win_ref, gkv_ref, wk_ref, wv_ref, kin_ref, vin_ref,
                     k_ref, v_ref):
    del kin_ref, vin_ref
    h = _norm(x_ref[0]) * (1.0 + sc_ref[0]) + sh_ref[0]
    z = _dot(h.astype(BF16), win_ref[...])
    _store_kv(z[:, :KV_LORA], z[:, KV_LORA:], gkv_ref, wk_ref, wv_ref, k_ref, v_ref, slice(None))


def _proj_call(x, sc, sh, cos_t, sin_t, w, n_keys, tm):
    b, n, d = x.shape
    nt = n // tm
    tok = lambda bi, i: (bi, i, 0)
    row = lambda bi, i: (bi, 0, 0)
    tab = lambda bi, i: (i, 0)
    hd4 = lambda bi, i: (bi, 0, i, 0)
    return pl.pallas_call(
        _proj_kernel,
        grid=(b, nt),
        in_specs=[pl.BlockSpec((1, tm, d), tok),
                  pl.BlockSpec((1, 1, d), row), pl.BlockSpec((1, 1, d), row),
                  pl.BlockSpec((tm, LANES), tab), pl.BlockSpec((tm, LANES), tab),
                  _const_spec(w["win"].shape), _const_spec(w["gq"].shape), _const_spec(w["wq"].shape),
                  _const_spec(w["gkv"].shape), _const_spec(w["wk"].shape), _const_spec(w["wv"].shape)],
        out_specs=[pl.BlockSpec((1, tm, C_CONV), tok),
                   pl.BlockSpec((1, N_HEADS, tm, LANES), hd4),
                   pl.BlockSpec((1, N_HEADS, tm, LANES), hd4),
                   pl.BlockSpec((1, HEAD_PAIRS, tm, V_TILE), hd4)],
        out_shape=[jax.ShapeDtypeStruct((b, n, C_CONV), F32),
                   jax.ShapeDtypeStruct((b, N_HEADS, n, LANES), BF16),
                   jax.ShapeDtypeStruct((b, N_HEADS, n_keys, LANES), BF16),
                   jax.ShapeDtypeStruct((b, HEAD_PAIRS, n_keys, V_TILE), BF16)],
        compiler_params=_params("parallel", "parallel"),
        name="proj",
    )(x, sc, sh, cos_t, sin_t, w["win"], w["gq"], w["wq"], w["gkv"], w["wk"], w["wv"])


def _proj_ctx_call(ctx, sc, sh, w, k_all, v_all, key_block):
    b, n_ctx, d = ctx.shape
    row = lambda bi: (0, 0, 0)
    kv = lambda bi: (bi, 0, key_block, 0)
    return pl.pallas_call(
        _proj_ctx_kernel,
        grid=(b,),
        in_specs=[pl.BlockSpec((1, n_ctx, d), lambda bi: (bi, 0, 0)),
                  pl.BlockSpec((1, 1, d), row), pl.BlockSpec((1, 1, d), row),
                  _const_spec(w["win_ctx"].shape), _const_spec(w["gkv"].shape),
                  _const_spec(w["wk"].shape), _const_spec(w["wv"].shape),
                  pl.BlockSpec(memory_space=pl.ANY), pl.BlockSpec(memory_space=pl.ANY)],
        out_specs=[pl.BlockSpec((1, N_HEADS, n_ctx, LANES), kv),
                   pl.BlockSpec((1, HEAD_PAIRS, n_ctx, V_TILE), kv)],
        out_shape=[jax.ShapeDtypeStruct(k_all.shape, BF16), jax.ShapeDtypeStruct(v_all.shape, BF16)],
        input_output_aliases={7: 0, 8: 1},
        compiler_params=_params("parallel"),
        name="proj_ctx",
    )(ctx, sc, sh, w["win_ctx"], w["gkv"], w["wk"], w["wv"], k_all, v_all)


def _attn_kernel(q_ref, k_ref, v_ref, o_ref):
    nk = k_ref.shape[2]
    n_chunks = nk // KEY_CHUNK
    n_sub = q_ref.shape[2] // ATTN_ROWS

    def scores(unit, c):
        sub, j = unit
        return lax.dot_general(q_ref[0, j, sub * ATTN_ROWS:(sub + 1) * ATTN_ROWS],
                               k_ref[0, j, c * KEY_CHUNK:(c + 1) * KEY_CHUNK],
                               (((1,), (1,)), ((), ())), preferred_element_type=F32)

    def row_max(s):
        m = s[0]
        for sc in s[1:]:
            m = jnp.maximum(m, sc)
        return jnp.max(m, axis=-1, keepdims=True)

    def weighted_values(s, m, c, acc):
        pv = _dot(jnp.exp2(s[c] - m).astype(BF16), v_ref[0, 0, c * KEY_CHUNK:(c + 1) * KEY_CHUNK])
        return pv if acc is None else acc + pv

    units = [(sub, j) for sub in range(n_sub) for j in range(2)]
    s_cur = [scores(units[0], c) for c in range(n_chunks)]
    outs = []
    for u in range(len(units)):
        m_cur, acc, s_nxt = row_max(s_cur), None, []
        for c in range(n_chunks):
            if u + 1 < len(units):
                s_nxt.append(scores(units[u + 1], c))
            acc = weighted_values(s_cur, m_cur, c, acc)
        outs.append(acc[:, :LANES] * (1.0 / acc[:, LANES:LANES + 1]))
        s_cur = s_nxt
    lane = lax.broadcasted_iota(jnp.int32, outs[0].shape, 1)
    for sub in range(n_sub):
        o_ref[0, sub * ATTN_ROWS:(sub + 1) * ATTN_ROWS] = jnp.where(
            lane < V_DIM, outs[2 * sub], outs[2 * sub + 1]).astype(BF16)


def _attn_call(q, k, v, tq):
    b, _, n, _ = q.shape
    nk = k.shape[2]
    return pl.pallas_call(
        _attn_kernel,
        grid=(b, HEAD_PAIRS, n // tq),
        in_specs=[pl.BlockSpec((1, 2, tq, LANES), lambda bi, p, i: (bi, p, i, 0)),
                  pl.BlockSpec((1, 2, nk, LANES), lambda bi, p, i: (bi, p, 0, 0)),
                  pl.BlockSpec((1, 1, nk, V_TILE), lambda bi, p, i: (bi, p, 0, 0))],
        out_specs=pl.BlockSpec((1, tq, LANES), lambda bi, p, i: (bi, i, p)),
        out_shape=jax.ShapeDtypeStruct((b, n, HEAD_PAIRS * LANES), BF16),
        compiler_params=_params("parallel", "parallel", "arbitrary"),
        name="attn",
    )(q, k, v)


def _halo_specs(tm, n, width):
    per = tm // HALO
    last = n // HALO - 1
    return (pl.BlockSpec((1, HALO, width), lambda bi, i: (bi, jnp.maximum(i * per - 1, 0), 0)),
            pl.BlockSpec((1, tm, width), lambda bi, i: (bi, i, 0)),
            pl.BlockSpec((1, HALO, width), lambda bi, i: (bi, jnp.minimum((i + 1) * per, last), 0)))


def _halo_masks():
    i = pl.program_id(1)
    has_prev = (i > 0).astype(F32)
    has_next = (i < pl.num_programs(1) - 1).astype(F32)
    return has_prev, has_next


CONV_ROWS = 32


def _mix_kernel(gp_ref, g_ref, gn_ref, a_ref, x_ref, g1_ref, dww_ref, dwb_ref, cg_ref, cb_ref,
                wo_ref, bo_ref, lg_ref, lb_ref, o_ref, win_ref, conv_ref):
    tm = g_ref.shape[1]
    has_prev, has_next = _halo_masks()
    for j in range(C_CONV // LANES):
        cols = slice(j * LANES, (j + 1) * LANES)
        win_ref[j, 0:HALO] = gp_ref[0, :, cols] * has_prev
        win_ref[j, HALO:HALO + tm] = g_ref[0, :, cols]
        win_ref[j, HALO + tm:] = gn_ref[0, :, cols] * has_next
    half = CONV_K // 2

    def rows(c, carry):
        base = pl.multiple_of(c * CONV_ROWS, CONV_ROWS)
        for j in range(C_CONV // LANES):
            cols = slice(j * LANES, (j + 1) * LANES)
            acc = jnp.zeros((CONV_ROWS, LANES), F32) + dwb_ref[:, cols]
            for kk in range(CONV_K):
                acc = acc + (win_ref[j, pl.ds(base + (HALO - half + kk), CONV_ROWS), :]
                             * dww_ref[kk:kk + 1, cols])
            conv_ref[pl.ds(base, CONV_ROWS), cols] = acc
        return carry

    lax.fori_loop(0, tm // CONV_ROWS, rows, 0)
    c = _silu(_norm(conv_ref[...]) * cg_ref[...] + cb_ref[...]).astype(BF16)
    y = _dot(c, wo_ref[:C_CONV, :]) + _dot(a_ref[0], wo_ref[C_CONV:, :]) + bo_ref[...]
    o_ref[0] = _norm(ALPHA * _norm(x_ref[0]) + g1_ref[0] * y) * lg_ref[...] + lb_ref[...]


def _mix_call(glu, attn, x, g1, w, tm):
    b, n, d = x.shape
    tok = lambda bi, i: (bi, i, 0)
    row = lambda bi, i: (bi, 0, 0)
    return pl.pallas_call(
        _mix_kernel,
        grid=(b, n // tm),
        in_specs=[*_halo_specs(tm, n, C_CONV),
                  pl.BlockSpec((1, tm, HEAD_PAIRS * LANES), tok),
                  pl.BlockSpec((1, tm, d), tok),
                  pl.BlockSpec((1, 1, d), row),
                  _const_spec(w["dww"].shape), _const_spec(w["dwb"].shape),
                  _const_spec(w["cg"].shape), _const_spec(w["cb"].shape),
                  _const_spec(w["wo"].shape), _const_spec(w["bo"].shape),
                  _const_spec(w["l1g"].shape), _const_spec(w["l1b"].shape)],
        out_specs=pl.BlockSpec((1, tm, d), tok),
        out_shape=jax.ShapeDtypeStruct((b, n, d), F32),
        scratch_shapes=[pltpu.VMEM((C_CONV // LANES, tm + 2 * HALO, LANES), F32),
                        pltpu.VMEM((tm, C_CONV), F32)],
        compiler_params=_params("parallel", "parallel"),
        name="mix",
    )(glu, glu, glu, attn, x, g1, w["dww"], w["dwb"], w["cg"], w["cb"], w["wo"], w["bo"],
      w["l1g"], w["l1b"])


def _ffn_kernel(xp_ref, x_ref, xn_ref, sc_ref, sh_ref, g2_ref, wup_ref, fw_ref, fb_ref, wdn_ref,
                bdn_ref, lg_ref, lb_ref, o_ref, h_ref, u_ref, act_ref):
    tm = x_ref.shape[1]
    has_prev, has_next = _halo_masks()
    scale = 1.0 + sc_ref[0]
    shift = sh_ref[0]
    h_ref[0:HALO] = ((xp_ref[0] * scale + shift) * has_prev).astype(BF16)
    h_ref[HALO:HALO + tm] = (x_ref[0] * scale + shift).astype(BF16)
    h_ref[HALO + tm:] = ((xn_ref[0] * scale + shift) * has_next).astype(BF16)

    groups = FF_CHUNK // LANES

    def conv(slot, col0, u):
        outs = []
        for j in range(groups):
            u_ref[slot + j] = u[:, j * LANES:(j + 1) * LANES]
            cols = slice(col0 + j * LANES, col0 + (j + 1) * LANES)
            acc = fb_ref[:, cols]
            for kk in range(FFN_K):
                acc = acc + u_ref[slot + j, pl.ds(HALO - 1 + kk, tm), :] * fw_ref[kk:kk + 1, cols]
            outs.append(acc)
        return outs

    for c in range(D_FF // FF_CHUNK):
        g = conv(4 * (c % 2), c * FF_CHUNK, _dot(h_ref[...], wup_ref[:, c * FF_CHUNK:(c + 1) * FF_CHUNK]))
        v = conv(4 * (c % 2) + 2, D_FF + c * FF_CHUNK,
                 _dot(h_ref[...], wup_ref[:, D_FF + c * FF_CHUNK:D_FF + (c + 1) * FF_CHUNK]))
        for j in range(groups):
            cols = slice(c * FF_CHUNK + j * LANES, c * FF_CHUNK + (j + 1) * LANES)
            act_ref[:, cols] = (_silu(g[j]) * v[j]).astype(BF16)

    f = _dot(act_ref[...], wdn_ref[...]) + bdn_ref[...]
    o_ref[0] = _norm(ALPHA * x_ref[0] + g2_ref[0] * f) * lg_ref[...] + lb_ref[...]


def _ffn_call(xm, sc, sh, g2, w, tm):
    b, n, d = xm.shape
    tok = lambda bi, i: (bi, i, 0)
    row = lambda bi, i: (bi, 0, 0)
    return pl.pallas_call(
        _ffn_kernel,
        grid=(b, n // tm),
        in_specs=[*_halo_specs(tm, n, d),
                  pl.BlockSpec((1, 1, d), row), pl.BlockSpec((1, 1, d), row), pl.BlockSpec((1, 1, d), row),
                  _const_spec(w["wup"].shape), _const_spec(w["fw"].shape), _const_spec(w["fb"].shape),
                  _const_spec(w["wdn"].shape), _const_spec(w["bdn"].shape),
                  _const_spec(w["l2g"].shape), _const_spec(w["l2b"].shape)],
        out_specs=pl.BlockSpec((1, tm, d), tok),
        out_shape=jax.ShapeDtypeStruct((b, n, d), F32),
        scratch_shapes=[pltpu.VMEM((tm + 2 * HALO, d), BF16),
                        pltpu.VMEM((4 * FF_CHUNK // LANES, tm + 2 * HALO, LANES), F32),
                        pltpu.VMEM((tm, D_FF), BF16)],
        compiler_params=_params("parallel", "parallel"),
        name="ffn",
    )(xm, xm, xm, sc, sh, g2, w["wup"], w["fw"], w["fb"], w["wdn"], w["bdn"], w["l2g"], w["l2b"])


def _rope_tables(n):
    t = jnp.arange(n)
    n_freq = QK_ROPE // 4
    inv = ROPE_THETA ** (-jnp.arange(n_freq, dtype=F32) / n_freq)
    ang = jnp.concatenate([(t // GRID_W)[:, None] * inv, (t % GRID_W)[:, None] * inv], axis=-1)
    cos, sin = jnp.cos(ang), jnp.sin(ang)
    pad = jnp.zeros((n, LANES - QK_NOPE - QK_ROPE), F32)
    cos_t = jnp.concatenate([jnp.ones((n, QK_NOPE), F32), cos, cos, pad], axis=-1)
    sin_t = jnp.concatenate([jnp.zeros((n, QK_NOPE), F32), sin, sin, pad], axis=-1)
    return cos_t, sin_t


def _partner(w_rope):
    half = w_rope.shape[-1] // 2
    return jnp.concatenate([-w_rope[..., half:], w_rope[..., :half]], axis=-1)


def _layout_weights(w_in, w_uq, w_ukv):
    d = w_in.shape[0]
    z = lambda r, c: jnp.zeros((r, c), F32)
    kr = w_in[:, _Z_KR:_Z_KR + QK_ROPE]
    tail = LANES - QK_NOPE - QK_ROPE
    win = jnp.concatenate([w_in[:, :_Z_KR], z(d, QK_NOPE), kr, _partner(kr)], axis=-1)
    win_ctx = jnp.concatenate([w_in[:, _Z_CKV:_Z_KR], z(d, QK_NOPE), kr, z(d, tail)], axis=-1)

    wq = w_uq.reshape(Q_LORA, N_HEADS, QK_NOPE + QK_ROPE)
    wq = jnp.concatenate([wq, _partner(wq[..., QK_NOPE:])], axis=-1).reshape(Q_LORA, N_HEADS * LANES)
    wkv = w_ukv.reshape(KV_LORA, N_HEADS, QK_NOPE + V_DIM)
    wk = jnp.concatenate([wkv[..., :QK_NOPE], jnp.zeros((KV_LORA, N_HEADS, LANES - QK_NOPE), F32)],
                         axis=-1).reshape(KV_LORA, N_HEADS * LANES)
    wv = wkv[..., QK_NOPE:].reshape(KV_LORA, N_HEADS * V_DIM)
    cast = lambda a: a.astype(BF16)
    return dict(win=cast(win), win_ctx=cast(win_ctx), wq=cast(wq), wk=cast(wk), wv=cast(wv))


def kernel(x, c, ctx, c_ctx, w_ada, b_ada, w_in, conv_dw_w, conv_dw_b, conv_ln_g, conv_ln_b, q_norm_g,
           w_uq, kv_norm_g, w_ukv, w_o, b_o, ln1_g, ln1_b, w_up, ffn_dw_w, ffn_dw_b, w_down, b_down,
           ln2_g, ln2_b):
    b, n, d = x.shape
    n_ctx = ctx.shape[1]
    n_keys = n + n_ctx
    tm = 512
    tq = 2 * ATTN_ROWS
    assert DEPTH == 1 and w_ada.shape[0] == 1
    assert n % tm == 0 and n % tq == 0 and n % n_ctx == 0 and tm % HALO == 0

    c_rows = jnp.concatenate([c, c_ctx[None, :], jnp.zeros((16 - b - 1, d), F32)], axis=0)
    mod = _mod_call(c_rows, w_ada[0], b_ada).reshape(16, 6, 1, d)
    sh1, sc1, g1, sh2, sc2, g2 = (mod[:b, j] for j in range(6))
    sh1c, sc1c = mod[b:b + 1, 0], mod[b:b + 1, 1]

    row2 = lambda a: a.reshape(1, -1)
    w = _layout_weights(w_in[0], w_uq[0], w_ukv[0])
    w.update(gq=row2(q_norm_g[0]), gkv=row2(kv_norm_g[0]))
    cos_t, sin_t = _rope_tables(n)

    glu, q, k_all, v_all = _proj_call(x, sc1, sh1, cos_t, sin_t, w, n_keys, 2 * PROJ_ROWS)
    k_all, v_all = _proj_ctx_call(ctx, sc1c, sh1c, w, k_all, v_all, n // n_ctx)
    attn = _attn_call(q, k_all, v_all, tq)

    wm = dict(dww=conv_dw_w[0], dwb=row2(conv_dw_b[0]), cg=row2(conv_ln_g[0]), cb=row2(conv_ln_b[0]),
              wo=w_o[0].astype(BF16), bo=row2(b_o[0]), l1g=row2(ln1_g[0]), l1b=row2(ln1_b[0]))
    x_mid = _mix_call(glu, attn, x, g1, wm, 2 * tm)

    wf = dict(wup=w_up[0].astype(BF16), fw=ffn_dw_w[0], fb=row2(ffn_dw_b[0]), wdn=w_down[0].astype(BF16),
              bdn=row2(b_down[0]), l2g=row2(ln2_g[0]), l2b=row2(ln2_b[0]))
    return _ffn_call(x_mid, sc2, sh2, g2, wf, tm)
```

```python
import math

import jax
import jax.numpy as jnp
from jax import lax
from jax.experimental import pallas as pl
from jax.experimental.pallas import tpu as pltpu

F32 = jnp.float32
BF16 = jnp.bfloat16

D_MODEL = 1024
GRID_W = 64
C_CONV = 512
CONV_K = 31
N_HEADS = 8
QK_NOPE = 64
QK_ROPE = 32
V_DIM = 64
Q_LORA = 384
KV_LORA = 256
D_FF = 2816
FFN_K = 3
ROPE_THETA = 10000.0
LN_EPS = 1e-5
DEPTH = 1
ALPHA = (2 * DEPTH) ** 0.25
SOFTMAX_SCALE = (QK_NOPE + QK_ROPE) ** -0.5
LOG2E = math.log2(math.e)

LANES = 128
SUBLANES = 8
HEAD_PAIRS = N_HEADS // 2
V_TILE = 2 * LANES
HALO = 16
FFN_HALO = SUBLANES
FF_CHUNK = 256
KEY_CHUNK = 256
ATTN_ROWS = 512
PROJ_ROWS = 512
VMEM_LIMIT = 52 * 1024 * 1024

_Z_CQ = 2 * C_CONV
_Z_CKV = _Z_CQ + Q_LORA
_Z_KR = _Z_CKV + KV_LORA
_Z_END = _Z_KR + LANES
PARTNER_ROLL = LANES - QK_ROPE


def _sigmoid(x):
    return 0.5 * jnp.tanh(0.5 * x) + 0.5


def _silu(x):
    h = 0.5 * x
    return h * jnp.tanh(h) + h


def _norm(x):
    mu = jnp.mean(x, axis=-1, keepdims=True)
    xc = x - mu
    var = jnp.mean(xc * xc, axis=-1, keepdims=True)
    return xc * lax.rsqrt(var + LN_EPS)


def _rms(x, g):
    return x * lax.rsqrt(jnp.mean(x * x, axis=-1, keepdims=True) + LN_EPS) * g


def _dot(a, b):
    return jnp.dot(a, b, preferred_element_type=F32)


def _params(*sem):
    return pltpu.CompilerParams(dimension_semantics=sem, vmem_limit_bytes=VMEM_LIMIT)


def _const_spec(shape):
    zeros = (0,) * len(shape)
    return pl.BlockSpec(shape, lambda *_: zeros)


def _mod_kernel(c_ref, w_ref, b_ref, o_ref):
    c = c_ref[...]
    a = _silu(c)
    o_ref[...] = jnp.dot(a, w_ref[...], precision=lax.Precision.HIGHEST,
                         preferred_element_type=F32) + b_ref[...]


def _mod_call(c_rows, w_ada, b_ada):
    rows, d = c_rows.shape
    n = w_ada.shape[1]
    bn = 1536
    return pl.pallas_call(
        _mod_kernel,
        grid=(n // bn,),
        in_specs=[pl.BlockSpec((rows, d), lambda j: (0, 0)),
                  pl.BlockSpec((d, bn), lambda j: (0, j)),
                  pl.BlockSpec((1, bn), lambda j: (0, j))],
        out_specs=pl.BlockSpec((rows, bn), lambda j: (0, j)),
        out_shape=jax.ShapeDtypeStruct((rows, n), F32),
        compiler_params=_params("arbitrary"),
        name="mod",
    )(c_rows, w_ada, b_ada)


def _store_kv(ckv, kr128, gkv_ref, wk_ref, wv_ref, k_ref, v_ref, rows):
    ckvn = _rms(ckv, gkv_ref[...]).astype(BF16)
    kk = _dot(ckvn, wk_ref[...])
    for h in range(N_HEADS):
        k_ref[0, h, rows] = (kk[:, h * LANES:(h + 1) * LANES] + kr128).astype(BF16)
    vv = _dot(ckvn, wv_ref[...])
    one_col = (lax.broadcasted_iota(jnp.int32, (vv.shape[0], LANES), 1) == 0).astype(BF16)
    for p in range(HEAD_PAIRS):
        v_ref[0, p, rows, :LANES] = vv[:, p * LANES:(p + 1) * LANES].astype(BF16)
        v_ref[0, p, rows, LANES:] = one_col


def _rotary(t, cos, sin):
    return t * cos + pltpu.roll(t, PARTNER_ROLL, 1) * sin


def _proj_kernel(x_ref, sc_ref, sh_ref, cos_ref, sin_ref, win_ref, gq_ref, wq_ref,
                 gkv_ref, wk_ref, wv_ref, glu_ref, q_ref, k_ref, v_ref):
    for r0 in range(0, x_ref.shape[1], PROJ_ROWS):
        rows = slice(r0, r0 + PROJ_ROWS)
        h = _norm(x_ref[0, rows]) * (1.0 + sc_ref[0]) + sh_ref[0]
        z = _dot(h.astype(BF16), win_ref[...])
        glu_ref[0, rows] = z[:, :C_CONV] * _sigmoid(z[:, C_CONV:_Z_CQ])
        cos = cos_ref[rows]
        sin = sin_ref[rows]
        cqn = _rms(z[:, _Z_CQ:_Z_CKV], gq_ref[...]).astype(BF16)
        qa = _dot(cqn, wq_ref[...])
        for hd in range(N_HEADS):
            q = _rotary(qa[:, hd * LANES:(hd + 1) * LANES], cos, sin) * (SOFTMAX_SCALE * LOG2E)
            q_ref[0, hd, rows] = q.astype(BF16)
        kr128 = _rotary(z[:, _Z_KR:_Z_END], cos, sin)
        _store_kv(z[:, _Z_CKV:_Z_KR], kr128, gkv_ref, wk_ref, wv_ref, k_ref, v_ref, rows)


def _proj_ctx_kernel(x_ref, sc_ref, sh_ref, win_ref, gkv_ref, wk_ref, wv_ref, k_ref, v_ref):
    h = _norm(x_ref[0]) * (1.0 + sc_ref[0]) + sh_ref[0]
    z = _dot(h.astype(BF16), win_ref[...])
    _store_kv(z[:, :KV_LORA], z[:, KV_LORA:], gkv_ref, wk_ref, wv_ref, k_ref, v_ref, slice(None))


def _proj_call(x, sc, sh, cos_t, sin_t, w, tm):
    b, n, d = x.shape
    nt = n // tm
    tok = lambda bi, i: (bi, i, 0)
    row = lambda bi, i: (bi, 0, 0)
    tab = lambda bi, i: (i, 0)
    hd4 = lambda bi, i: (bi, 0, i, 0)
    return pl.pallas_call(
        _proj_kernel,
        grid=(b, nt),
        in_specs=[pl.BlockSpec((1, tm, d), tok),
                  pl.BlockSpec((1, 1, d), row), pl.BlockSpec((1, 1, d), row),
                  pl.BlockSpec((tm, LANES), tab), pl.BlockSpec((tm, LANES), tab),
                  _const_spec(w["win"].shape), _const_spec(w["gq"].shape), _const_spec(w["wq"].shape),
                  _const_spec(w["gkv"].shape), _const_spec(w["wk"].shape), _const_spec(w["wv"].shape)],
        out_specs=[pl.BlockSpec((1, tm, C_CONV), tok),
                   pl.BlockSpec((1, N_HEADS, tm, LANES), hd4),
                   pl.BlockSpec((1, N_HEADS, tm, LANES), hd4),
                   pl.BlockSpec((1, HEAD_PAIRS, tm, V_TILE), hd4)],
        out_shape=[jax.ShapeDtypeStruct((b, n, C_CONV), F32),
                   jax.ShapeDtypeStruct((b, N_HEADS, n, LANES), BF16),
                   jax.ShapeDtypeStruct((b, N_HEADS, n, LANES), BF16),
                   jax.ShapeDtypeStruct((b, HEAD_PAIRS, n, V_TILE), BF16)],
        compiler_params=_params("parallel", "parallel"),
        name="proj",
    )(x, sc, sh, cos_t, sin_t, w["win"], w["gq"], w["wq"], w["gkv"], w["wk"], w["wv"])


def _proj_ctx_call(ctx, sc, sh, w):
    b, n_ctx, d = ctx.shape
    row = lambda bi: (0, 0, 0)
    kv = lambda bi: (bi, 0, 0, 0)
    return pl.pallas_call(
        _proj_ctx_kernel,
        grid=(b,),
        in_specs=[pl.BlockSpec((1, n_ctx, d), lambda bi: (bi, 0, 0)),
                  pl.BlockSpec((1, 1, d), row), pl.BlockSpec((1, 1, d), row),
                  _const_spec(w["win_ctx"].shape), _const_spec(w["gkv"].shape),
                  _const_spec(w["wk"].shape), _const_spec(w["wv"].shape)],
        out_specs=[pl.BlockSpec((1, N_HEADS, n_ctx, LANES), kv),
                   pl.BlockSpec((1, HEAD_PAIRS, n_ctx, V_TILE), kv)],
        out_shape=[jax.ShapeDtypeStruct((b, N_HEADS, n_ctx, LANES), BF16),
                   jax.ShapeDtypeStruct((b, HEAD_PAIRS, n_ctx, V_TILE), BF16)],
        compiler_params=_params("parallel"),
        name="proj_ctx",
    )(ctx, sc, sh, w["win_ctx"], w["gkv"], w["wk"], w["wv"])


def _attn_kernel(q_ref, k_ref, v_ref, kc_ref, vc_ref, o_ref):
    chunks = [(k_ref, v_ref, c0) for c0 in range(0, k_ref.shape[2], KEY_CHUNK)]
    chunks += [(kc_ref, vc_ref, c0) for c0 in range(0, kc_ref.shape[2], KEY_CHUNK)]
    n_chunks = len(chunks)
    n_sub = q_ref.shape[2] // ATTN_ROWS

    def scores(unit, c):
        sub, j = unit
        keys, _, c0 = chunks[c]
        return lax.dot_general(q_ref[0, j, sub * ATTN_ROWS:(sub + 1) * ATTN_ROWS],
                               keys[0, j, c0:c0 + KEY_CHUNK],
                               (((1,), (1,)), ((), ())), preferred_element_type=F32)

    def row_max(s):
        m = s[0]
        for sc in s[1:]:
            m = jnp.maximum(m, sc)
        return jnp.max(m, axis=-1, keepdims=True)

    def weighted_values(s, m, c, acc):
        _, values, c0 = chunks[c]
        pv = _dot(jnp.exp2(s[c] - m).astype(BF16), values[0, 0, c0:c0 + KEY_CHUNK])
        return pv if acc is None else acc + pv

    units = [(sub, j) for sub in range(n_sub) for j in range(2)]
    s_cur = [scores(units[0], c) for c in range(n_chunks)]
    outs = []
    for u in range(len(units)):
        m_cur, acc, s_nxt = row_max(s_cur), None, []
        for c in range(n_chunks):
            if u + 1 < len(units):
                s_nxt.append(scores(units[u + 1], c))
            acc = weighted_values(s_cur, m_cur, c, acc)
        outs.append(acc[:, :LANES] * (1.0 / acc[:, LANES:LANES + 1]))
        s_cur = s_nxt
    lane = lax.broadcasted_iota(jnp.int32, outs[0].shape, 1)
    for sub in range(n_sub):
        o_ref[0, sub * ATTN_ROWS:(sub + 1) * ATTN_ROWS] = jnp.where(
            lane < V_DIM, outs[2 * sub], outs[2 * sub + 1]).astype(BF16)


def _attn_call(q, k, v, kc, vc, tq):
    b, _, n, _ = q.shape
    nk, nc = k.shape[2], kc.shape[2]
    assert nk % KEY_CHUNK == 0 and nc % KEY_CHUNK == 0
    return pl.pallas_call(
        _attn_kernel,
        grid=(b, HEAD_PAIRS, n // tq),
        in_specs=[pl.BlockSpec((1, 2, tq, LANES), lambda bi, p, i: (bi, p, i, 0)),
                  pl.BlockSpec((1, 2, nk, LANES), lambda bi, p, i: (bi, p, 0, 0)),
                  pl.BlockSpec((1, 1, nk, V_TILE), lambda bi, p, i: (bi, p, 0, 0)),
                  pl.BlockSpec((1, 2, nc, LANES), lambda bi, p, i: (bi, p, 0, 0)),
                  pl.BlockSpec((1, 1, nc, V_TILE), lambda bi, p, i: (bi, p, 0, 0))],
        out_specs=pl.BlockSpec((1, tq, LANES), lambda bi, p, i: (bi, i, p)),
        out_shape=jax.ShapeDtypeStruct((b, n, HEAD_PAIRS * LANES), BF16),
        compiler_params=_params("parallel", "parallel", "arbitrary"),
        name="attn",
    )(q, k, v, kc, vc)


def _halo_specs(tm, n, width, halo):
    per = tm // halo
    last = n // halo - 1
    return (pl.BlockSpec((1, halo, width), lambda bi, i: (bi, jnp.maximum(i * per - 1, 0), 0)),
            pl.BlockSpec((1, tm, width), lambda bi, i: (bi, i, 0)),
            pl.BlockSpec((1, halo, width), lambda bi, i: (bi, jnp.minimum((i + 1) * per, last), 0)))


def _halo_masks():
    i = pl.program_id(1)
    has_prev = (i > 0).astype(F32)
    has_next = (i < pl.num_programs(1) - 1).astype(F32)
    return has_prev, has_next


CONV_ROWS = 32


def _mix_kernel(gp_ref, g_ref, gn_ref, a_ref, x_ref, g1_ref, dww_ref, dwb_ref, cg_ref, cb_ref,
                wo_ref, bo_ref, lg_ref, lb_ref, o_ref, win_ref, conv_ref):
    tm = g_ref.shape[1]
    has_prev, has_next = _halo_masks()
    for j in range(C_CONV // LANES):
        cols = slice(j * LANES, (j + 1) * LANES)
        win_ref[j, 0:HALO] = gp_ref[0, :, cols] * has_prev
        win_ref[j, HALO:HALO + tm] = g_ref[0, :, cols]
        win_ref[j, HALO + tm:] = gn_ref[0, :, cols] * has_next
    half = CONV_K // 2

    def rows(c, carry):
        base = pl.multiple_of(c * CONV_ROWS, CONV_ROWS)
        for j in range(C_CONV // LANES):
            cols = slice(j * LANES, (j + 1) * LANES)
            acc = jnp.zeros((CONV_ROWS, LANES), F32) + dwb_ref[:, cols]
            for kk in range(CONV_K):
                acc = acc + (win_ref[j, pl.ds(base + (HALO - half + kk), CONV_ROWS), :]
                             * dww_ref[kk:kk + 1, cols])
            conv_ref[pl.ds(base, CONV_ROWS), cols] = acc
        return carry

    lax.fori_loop(0, tm // CONV_ROWS, rows, 0)
    c = _silu(_norm(conv_ref[...]) * cg_ref[...] + cb_ref[...]).astype(BF16)
    y = _dot(c, wo_ref[:C_CONV, :]) + _dot(a_ref[0], wo_ref[C_CONV:, :]) + bo_ref[...]
    o_ref[0] = _norm(ALPHA * _norm(x_ref[0]) + g1_ref[0] * y) * lg_ref[...] + lb_ref[...]


def _mix_call(glu, attn, x, g1, w, tm):
    b, n, d = x.shape
    tok = lambda bi, i: (bi, i, 0)
    row = lambda bi, i: (bi, 0, 0)
    return pl.pallas_call(
        _mix_kernel,
        grid=(b, n // tm),
        in_specs=[*_halo_specs(tm, n, C_CONV, HALO),
                  pl.BlockSpec((1, tm, HEAD_PAIRS * LANES), tok),
                  pl.BlockSpec((1, tm, d), tok),
                  pl.BlockSpec((1, 1, d), row),
                  _const_spec(w["dww"].shape), _const_spec(w["dwb"].shape),
                  _const_spec(w["cg"].shape), _const_spec(w["cb"].shape),
                  _const_spec(w["wo"].shape), _const_spec(w["bo"].shape),
                  _const_spec(w["l1g"].shape), _const_spec(w["l1b"].shape)],
        out_specs=pl.BlockSpec((1, tm, d), tok),
        out_shape=jax.ShapeDtypeStruct((b, n, d), F32),
        scratch_shapes=[pltpu.VMEM((C_CONV // LANES, tm + 2 * HALO, LANES), F32),
                        pltpu.VMEM((tm, C_CONV), F32)],
        compiler_params=_params("parallel", "parallel"),
        name="mix",
    )(glu, glu, glu, attn, x, g1, w["dww"], w["dwb"], w["cg"], w["cb"], w["wo"], w["bo"],
      w["l1g"], w["l1b"])


def _ffn_kernel(xp_ref, x_ref, xn_ref, sc_ref, sh_ref, g2_ref, wup_ref, fw_ref, fb_ref, wdn_ref,
                bdn_ref, lg_ref, lb_ref, o_ref, h_ref, u_ref, act_ref):
    tm = x_ref.shape[1]
    has_prev, has_next = _halo_masks()
    scale = 1.0 + sc_ref[0]
    shift = sh_ref[0]
    h_ref[...] = jnp.concatenate([(xp_ref[0] * scale + shift) * has_prev,
                                  x_ref[0] * scale + shift,
                                  (xn_ref[0] * scale + shift) * has_next], axis=0).astype(BF16)

    groups = FF_CHUNK // LANES

    def conv(slot, col0, u):
        outs = []
        for j in range(groups):
            u_ref[slot + j] = u[:, j * LANES:(j + 1) * LANES]
            cols = slice(col0 + j * LANES, col0 + (j + 1) * LANES)
            acc = fb_ref[:, cols]
            for kk in range(FFN_K):
                acc = acc + u_ref[slot + j, pl.ds(FFN_HALO - 1 + kk, tm), :] * fw_ref[kk:kk + 1, cols]
            outs.append(acc)
        return outs

    for c in range(D_FF // FF_CHUNK):
        g = conv(4 * (c % 2), c * FF_CHUNK, _dot(h_ref[...], wup_ref[:, c * FF_CHUNK:(c + 1) * FF_CHUNK]))
        v = conv(4 * (c % 2) + 2, D_FF + c * FF_CHUNK,
                 _dot(h_ref[...], wup_ref[:, D_FF + c * FF_CHUNK:D_FF + (c + 1) * FF_CHUNK]))
        for j in range(groups):
            cols = slice(c * FF_CHUNK + j * LANES, c * FF_CHUNK + (j + 1) * LANES)
            act_ref[:, cols] = (_silu(g[j]) * v[j]).astype(BF16)

    f = _dot(act_ref[...], wdn_ref[...]) + bdn_ref[...]
    o_ref[0] = _norm(ALPHA * x_ref[0] + g2_ref[0] * f) * lg_ref[...] + lb_ref[...]


def _ffn_call(xm, sc, sh, g2, w, tm):
    b, n, d = xm.shape
    tok = lambda bi, i: (bi, i, 0)
    row = lambda bi, i: (bi, 0, 0)
    return pl.pallas_call(
        _ffn_kernel,
        grid=(b, n // tm),
        in_specs=[*_halo_specs(tm, n, d, FFN_HALO),
                  pl.BlockSpec((1, 1, d), row), pl.BlockSpec((1, 1, d), row), pl.BlockSpec((1, 1, d), row),
                  _const_spec(w["wup"].shape), _const_spec(w["fw"].shape), _const_spec(w["fb"].shape),
                  _const_spec(w["wdn"].shape), _const_spec(w["bdn"].shape),
                  _const_spec(w["l2g"].shape), _const_spec(w["l2b"].shape)],
        out_specs=pl.BlockSpec((1, tm, d), tok),
        out_shape=jax.ShapeDtypeStruct((b, n, d), F32),
        scratch_shapes=[pltpu.VMEM((tm + 2 * FFN_HALO, d), BF16),
                        pltpu.VMEM((4 * FF_CHUNK // LANES, tm + 2 * FFN_HALO, LANES), F32),
                        pltpu.VMEM((tm, D_FF), BF16)],
        compiler_params=_params("parallel", "parallel"),
        name="ffn",
    )(xm, xm, xm, sc, sh, g2, w["wup"], w["fw"], w["fb"], w["wdn"], w["bdn"], w["l2g"], w["l2b"])


def _rope_tables(n):
    t = jnp.arange(n)
    n_freq = QK_ROPE // 4
    inv = ROPE_THETA ** (-jnp.arange(n_freq, dtype=F32) / n_freq)
    ang = jnp.concatenate([(t // GRID_W)[:, None] * inv, (t % GRID_W)[:, None] * inv], axis=-1)
    cos, sin = jnp.cos(ang), jnp.sin(ang)
    pad = jnp.zeros((n, LANES - QK_NOPE - QK_ROPE), F32)
    cos_t = jnp.concatenate([jnp.ones((n, QK_NOPE), F32), cos, cos, pad], axis=-1)
    sin_t = jnp.concatenate([jnp.zeros((n, QK_NOPE), F32), sin, sin, pad], axis=-1)
    return cos_t, sin_t


def _partner(w_rope):
    half = w_rope.shape[-1] // 2
    return jnp.concatenate([-w_rope[..., half:], w_rope[..., :half]], axis=-1)


def _layout_weights(w_in, w_uq, w_ukv):
    d = w_in.shape[0]
    tail = LANES - QK_NOPE - QK_ROPE
    w_in = w_in.astype(BF16)
    z = lambda r, c: jnp.zeros((r, c), BF16)
    kr = w_in[:, _Z_KR:_Z_KR + QK_ROPE]
    win = jnp.concatenate([w_in[:, :_Z_KR], z(d, QK_NOPE), kr, _partner(kr)], axis=-1)
    win_ctx = jnp.concatenate([w_in[:, _Z_CKV:_Z_KR], z(d, QK_NOPE), kr, z(d, tail)], axis=-1)

    wq = w_uq.reshape(Q_LORA, N_HEADS, QK_NOPE + QK_ROPE)
    wq = jnp.concatenate([wq, _partner(wq[..., QK_NOPE:])], axis=-1).reshape(Q_LORA, N_HEADS * LANES)
    wkv = w_ukv.reshape(KV_LORA, N_HEADS, QK_NOPE + V_DIM)
    wk = jnp.concatenate([wkv[..., :QK_NOPE], jnp.zeros((KV_LORA, N_HEADS, LANES - QK_NOPE), F32)],
                         axis=-1).reshape(KV_LORA, N_HEADS * LANES)
    wv = wkv[..., QK_NOPE:].reshape(KV_LORA, N_HEADS * V_DIM)
    cast = lambda a: a.astype(BF16)
    return dict(win=cast(win), win_ctx=cast(win_ctx), wq=cast(wq), wk=cast(wk), wv=cast(wv))


def kernel(x, c, ctx, c_ctx, w_ada, b_ada, w_in, conv_dw_w, conv_dw_b, conv_ln_g, conv_ln_b, q_norm_g,
           w_uq, kv_norm_g, w_ukv, w_o, b_o, ln1_g, ln1_b, w_up, ffn_dw_w, ffn_dw_b, w_down, b_down,
           ln2_g, ln2_b):
    b, n, d = x.shape
    tm = 512
    tq = 2 * ATTN_ROWS
    assert DEPTH == 1 and w_ada.shape[0] == 1
    assert n % tm == 0 and n % tq == 0 and tm % HALO == 0

    c_rows = jnp.concatenate([c, c_ctx[None, :], jnp.zeros((16 - b - 1, d), F32)], axis=0)
    mod = _mod_call(c_rows, w_ada[0], b_ada).reshape(16, 6, 1, d)
    sh1, sc1, g1, sh2, sc2, g2 = (mod[:b, j] for j in range(6))
    sh1c, sc1c = mod[b:b + 1, 0], mod[b:b + 1, 1]

    row2 = lambda a: a.reshape(1, -1)
    w = _layout_weights(w_in[0], w_uq[0], w_ukv[0])
    w.update(gq=row2(q_norm_g[0]), gkv=row2(kv_norm_g[0]))
    cos_t, sin_t = _rope_tables(n)

    glu, q, k_lat, v_lat = _proj_call(x, sc1, sh1, cos_t, sin_t, w, 2 * PROJ_ROWS)
    k_ctx, v_ctx = _proj_ctx_call(ctx, sc1c, sh1c, w)
    attn = _attn_call(q, k_lat, v_lat, k_ctx, v_ctx, tq)

    wm = dict(dww=conv_dw_w[0], dwb=row2(conv_dw_b[0]), cg=row2(conv_ln_g[0]), cb=row2(conv_ln_b[0]),
              wo=w_o[0].astype(BF16), bo=row2(b_o[0]), l1g=row2(ln1_g[0]), l1b=row2(ln1_b[0]))
    x_mid = _mix_call(glu, attn, x, g1, wm, 2 * tm)

    wf = dict(wup=w_up[0].astype(BF16), fw=ffn_dw_w[0], fb=row2(ffn_dw_b[0]), wdn=w_down[0].astype(BF16),
              bdn=row2(b_down[0]), l2g=row2(ln2_g[0]), l2b=row2(ln2_b[0]))
    return _ffn_call(x_mid, sc2, sh2, g2, wf, tm)
```

```python
import math

import jax
import jax.numpy as jnp
from jax import lax
from jax.experimental import pallas as pl
from jax.experimental.pallas import tpu as pltpu

F32 = jnp.float32
BF16 = jnp.bfloat16

D_MODEL = 1024
GRID_W = 64
C_CONV = 512
CONV_K = 31
N_HEADS = 8
QK_NOPE = 64
QK_ROPE = 32
V_DIM = 64
Q_LORA = 384
KV_LORA = 256
D_FF = 2816
FFN_K = 3
ROPE_THETA = 10000.0
LN_EPS = 1e-5
DEPTH = 1
ALPHA = (2 * DEPTH) ** 0.25
SOFTMAX_SCALE = (QK_NOPE + QK_ROPE) ** -0.5
LOG2E = math.log2(math.e)

LANES = 128
SUBLANES = 8
HEAD_PAIRS = N_HEADS // 2
V_TILE = 2 * LANES
HALO = 16
FFN_HALO = SUBLANES
FF_CHUNK = 256
KEY_CHUNK = 256
ATTN_ROWS = 512
PROJ_ROWS = 512
VMEM_LIMIT = 52 * 1024 * 1024

_Z_CQ = 2 * C_CONV
_Z_CKV = _Z_CQ + Q_LORA
_Z_KR = _Z_CKV + KV_LORA
_Z_END = _Z_KR + LANES
PARTNER_ROLL = LANES - QK_ROPE


def _sigmoid(x):
    return 0.5 * jnp.tanh(0.5 * x) + 0.5


def _silu(x):
    h = 0.5 * x
    return h * jnp.tanh(h) + h


def _norm(x, gain=1.0):
    mu = jnp.mean(x, axis=-1, keepdims=True)
    xc = x - mu
    var = jnp.mean(xc * xc, axis=-1, keepdims=True)
    return xc * (lax.rsqrt(var + LN_EPS) * gain)


def _rms(x, g):
    return x * lax.rsqrt(jnp.mean(x * x, axis=-1, keepdims=True) + LN_EPS) * g


def _dot(a, b):
    return jnp.dot(a, b, preferred_element_type=F32)


def _params(*sem):
    return pltpu.CompilerParams(dimension_semantics=sem, vmem_limit_bytes=VMEM_LIMIT)


def _const_spec(shape):
    zeros = (0,) * len(shape)
    return pl.BlockSpec(shape, lambda *_: zeros)


def _mod_kernel(c_ref, w_ref, b_ref, o_ref):
    c = c_ref[...]
    a = _silu(c)
    o_ref[...] = jnp.dot(a, w_ref[...], precision=lax.Precision.HIGHEST,
                         preferred_element_type=F32) + b_ref[...]


def _mod_call(c_rows, w_ada, b_ada):
    rows, d = c_rows.shape
    n = w_ada.shape[1]
    bn = 1536
    return pl.pallas_call(
        _mod_kernel,
        grid=(n // bn,),
        in_specs=[pl.BlockSpec((rows, d), lambda j: (0, 0)),
                  pl.BlockSpec((d, bn), lambda j: (0, j)),
                  pl.BlockSpec((1, bn), lambda j: (0, j))],
        out_specs=pl.BlockSpec((rows, bn), lambda j: (0, j)),
        out_shape=jax.ShapeDtypeStruct((rows, n), F32),
        compiler_params=_params("arbitrary"),
        name="mod",
    )(c_rows, w_ada, b_ada)


def _store_kv(ckv, kr128, gkv_ref, wk_ref, wv_ref, k_ref, v_ref, rows):
    ckvn = _rms(ckv, gkv_ref[...]).astype(BF16)
    kk = _dot(ckvn, wk_ref[...])
    for h in range(N_HEADS):
        k_ref[0, h, rows] = (kk[:, h * LANES:(h + 1) * LANES] + kr128).astype(BF16)
    vv = _dot(ckvn, wv_ref[...])
    one_col = (lax.broadcasted_iota(jnp.int32, (vv.shape[0], LANES), 1) == 0).astype(BF16)
    for p in range(HEAD_PAIRS):
        v_ref[0, p, rows, :LANES] = vv[:, p * LANES:(p + 1) * LANES].astype(BF16)
        v_ref[0, p, rows, LANES:] = one_col


def _rotary(t, cos, sin):
    return t * cos + pltpu.roll(t, PARTNER_ROLL, 1) * sin


def _proj_kernel(x_ref, sc_ref, sh_ref, cos_ref, sin_ref, win_ref, gq_ref, wq_ref,
                 gkv_ref, wk_ref, wv_ref, glu_ref, q_ref, k_ref, v_ref):
    for r0 in range(0, x_ref.shape[1], PROJ_ROWS):
        rows = slice(r0, r0 + PROJ_ROWS)
        h = _norm(x_ref[0, rows]) * (1.0 + sc_ref[0]) + sh_ref[0]
        z = _dot(h.astype(BF16), win_ref[...])
        glu_ref[0, rows] = z[:, :C_CONV] * _sigmoid(z[:, C_CONV:_Z_CQ])
        cos = cos_ref[rows]
        sin = sin_ref[rows]
        cqn = _rms(z[:, _Z_CQ:_Z_CKV], gq_ref[...]).astype(BF16)
        qa = _dot(cqn, wq_ref[...])
        for hd in range(N_HEADS):
            q = _rotary(qa[:, hd * LANES:(hd + 1) * LANES], cos, sin) * (SOFTMAX_SCALE * LOG2E)
            q_ref[0, hd, rows] = q.astype(BF16)
        kr128 = _rotary(z[:, _Z_KR:_Z_END], cos, sin)
        _store_kv(z[:, _Z_CKV:_Z_KR], kr128, gkv_ref, wk_ref, wv_ref, k_ref, v_ref, rows)


def _proj_ctx_kernel(x_ref, sc_ref, sh_ref, win_ref, gkv_ref, wk_ref, wv_ref, k_ref, v_ref):
    h = _norm(x_ref[0]) * (1.0 + sc_ref[0]) + sh_ref[0]
    z = _dot(h.astype(BF16), win_ref[...])
    _store_kv(z[:, :KV_LORA], z[:, KV_LORA:], gkv_ref, wk_ref, wv_ref, k_ref, v_ref, slice(None))


def _proj_call(x, sc, sh, cos_t, sin_t, w, tm):
    b, n, d = x.shape
    nt = n // tm
    tok = lambda bi, i: (bi, i, 0)
    row = lambda bi, i: (bi, 0, 0)
    tab = lambda bi, i: (i, 0)
    hd4 = lambda bi, i: (bi, 0, i, 0)
    return pl.pallas_call(
        _proj_kernel,
        grid=(b, nt),
        in_specs=[pl.BlockSpec((1, tm, d), tok),
                  pl.BlockSpec((1, 1, d), row), pl.BlockSpec((1, 1, d), row),
                  pl.BlockSpec((tm, LANES), tab), pl.BlockSpec((tm, LANES), tab),
                  _const_spec(w["win"].shape), _const_spec(w["gq"].shape), _const_spec(w["wq"].shape),
                  _const_spec(w["gkv"].shape), _const_spec(w["wk"].shape), _const_spec(w["wv"].shape)],
        out_specs=[pl.BlockSpec((1, tm, C_CONV), tok),
                   pl.BlockSpec((1, N_HEADS, tm, LANES), hd4),
                   pl.BlockSpec((1, N_HEADS, tm, LANES), hd4),
                   pl.BlockSpec((1, HEAD_PAIRS, tm, V_TILE), hd4)],
        out_shape=[jax.ShapeDtypeStruct((b, n, C_CONV), F32),
                   jax.ShapeDtypeStruct((b, N_HEADS, n, LANES), BF16),
                   jax.ShapeDtypeStruct((b, N_HEADS, n, LANES), BF16),
                   jax.ShapeDtypeStruct((b, HEAD_PAIRS, n, V_TILE), BF16)],
        compiler_params=_params("parallel", "parallel"),
        name="proj",
    )(x, sc, sh, cos_t, sin_t, w["win"], w["gq"], w["wq"], w["gkv"], w["wk"], w["wv"])


def _proj_ctx_call(ctx, sc, sh, w):
    b, n_ctx, d = ctx.shape
    row = lambda bi: (0, 0, 0)
    kv = lambda bi: (bi, 0, 0, 0)
    return pl.pallas_call(
        _proj_ctx_kernel,
        grid=(b,),
        in_specs=[pl.BlockSpec((1, n_ctx, d), lambda bi: (bi, 0, 0)),
                  pl.BlockSpec((1, 1, d), row), pl.BlockSpec((1, 1, d), row),
                  _const_spec(w["win_ctx"].shape), _const_spec(w["gkv"].shape),
                  _const_spec(w["wk"].shape), _const_spec(w["wv"].shape)],
        out_specs=[pl.BlockSpec((1, N_HEADS, n_ctx, LANES), kv),
                   pl.BlockSpec((1, HEAD_PAIRS, n_ctx, V_TILE), kv)],
        out_shape=[jax.ShapeDtypeStruct((b, N_HEADS, n_ctx, LANES), BF16),
                   jax.ShapeDtypeStruct((b, HEAD_PAIRS, n_ctx, V_TILE), BF16)],
        compiler_params=_params("parallel"),
        name="proj_ctx",
    )(ctx, sc, sh, w["win_ctx"], w["gkv"], w["wk"], w["wv"])


def _attn_kernel(q_ref, k_ref, v_ref, kc_ref, vc_ref, o_ref):
    chunks = [(k_ref, v_ref, c0) for c0 in range(0, k_ref.shape[2], KEY_CHUNK)]
    chunks += [(kc_ref, vc_ref, c0) for c0 in range(0, kc_ref.shape[2], KEY_CHUNK)]
    n_chunks = len(chunks)
    n_sub = q_ref.shape[2] // ATTN_ROWS

    def scores(unit, c):
        sub, j = unit
        keys, _, c0 = chunks[c]
        return lax.dot_general(q_ref[0, j, sub * ATTN_ROWS:(sub + 1) * ATTN_ROWS],
                               keys[0, j, c0:c0 + KEY_CHUNK],
                               (((1,), (1,)), ((), ())), preferred_element_type=F32)

    def row_max(s):
        m = s[0]
        for sc in s[1:]:
            m = jnp.maximum(m, sc)
        return jnp.max(m, axis=-1, keepdims=True)

    def weighted_values(s, m, c, acc):
        _, values, c0 = chunks[c]
        pv = _dot(jnp.exp2(s[c] - m).astype(BF16), values[0, 0, c0:c0 + KEY_CHUNK])
        return pv if acc is None else acc + pv

    units = [(sub, j) for sub in range(n_sub) for j in range(2)]
    s_cur = [scores(units[0], c) for c in range(n_chunks)]
    outs = []
    for u in range(len(units)):
        m_cur, acc, s_nxt = row_max(s_cur), None, []
        for c in range(n_chunks):
            if u + 1 < len(units):
                s_nxt.append(scores(units[u + 1], c))
            acc = weighted_values(s_cur, m_cur, c, acc)
        outs.append(acc[:, :LANES] * (1.0 / acc[:, LANES:LANES + 1]))
        s_cur = s_nxt
    lane = lax.broadcasted_iota(jnp.int32, outs[0].shape, 1)
    for sub in range(n_sub):
        o_ref[0, sub * ATTN_ROWS:(sub + 1) * ATTN_ROWS] = jnp.where(
            lane < V_DIM, outs[2 * sub], outs[2 * sub + 1]).astype(BF16)


def _attn_call(q, k, v, kc, vc, tq):
    b, _, n, _ = q.shape
    nk, nc = k.shape[2], kc.shape[2]
    assert nk % KEY_CHUNK == 0 and nc % KEY_CHUNK == 0
    return pl.pallas_call(
        _attn_kernel,
        grid=(b, HEAD_PAIRS, n // tq),
        in_specs=[pl.BlockSpec((1, 2, tq, LANES), lambda bi, p, i: (bi, p, i, 0)),
                  pl.BlockSpec((1, 2, nk, LANES), lambda bi, p, i: (bi, p, 0, 0)),
                  pl.BlockSpec((1, 1, nk, V_TILE), lambda bi, p, i: (bi, p, 0, 0)),
                  pl.BlockSpec((1, 2, nc, LANES), lambda bi, p, i: (bi, p, 0, 0)),
                  pl.BlockSpec((1, 1, nc, V_TILE), lambda bi, p, i: (bi, p, 0, 0))],
        out_specs=pl.BlockSpec((1, tq, LANES), lambda bi, p, i: (bi, i, p)),
        out_shape=jax.ShapeDtypeStruct((b, n, HEAD_PAIRS * LANES), BF16),
        compiler_params=_params("parallel", "parallel", "arbitrary"),
        name="attn",
    )(q, k, v, kc, vc)


def _halo_specs(tm, n, width, halo):
    per = tm // halo
    last = n // halo - 1
    return (pl.BlockSpec((1, halo, width), lambda bi, i: (bi, jnp.maximum(i * per - 1, 0), 0)),
            pl.BlockSpec((1, tm, width), lambda bi, i: (bi, i, 0)),
            pl.BlockSpec((1, halo, width), lambda bi, i: (bi, jnp.minimum((i + 1) * per, last), 0)))


def _halo_masks():
    i = pl.program_id(1)
    has_prev = (i > 0).astype(F32)
    has_next = (i < pl.num_programs(1) - 1).astype(F32)
    return has_prev, has_next


CONV_ROWS = 64


def _mix_kernel(gp_ref, g_ref, gn_ref, a_ref, x_ref, g1_ref, dww_ref, dwb_ref, cg_ref, cb_ref,
                wo_ref, bo_ref, lg_ref, lb_ref, o_ref, win_ref, conv_ref):
    tm = g_ref.shape[1]
    has_prev, has_next = _halo_masks()
    for j in range(C_CONV // LANES):
        cols = slice(j * LANES, (j + 1) * LANES)
        win_ref[j, 0:HALO] = gp_ref[0, :, cols] * has_prev
        win_ref[j, HALO:HALO + tm] = g_ref[0, :, cols]
        win_ref[j, HALO + tm:] = gn_ref[0, :, cols] * has_next
    half = CONV_K // 2

    def rows(c, carry):
        base = pl.multiple_of(c * CONV_ROWS, CONV_ROWS)
        for j in range(C_CONV // LANES):
            cols = slice(j * LANES, (j + 1) * LANES)
            acc = jnp.zeros((CONV_ROWS, LANES), F32) + dwb_ref[:, cols]
            for kk in range(CONV_K):
                acc = acc + (win_ref[j, pl.ds(base + (HALO - half + kk), CONV_ROWS), :]
                             * dww_ref[kk:kk + 1, cols])
            conv_ref[pl.ds(base, CONV_ROWS), cols] = acc
        return carry

    lax.fori_loop(0, tm // CONV_ROWS, rows, 0)
    c = _silu(_norm(conv_ref[...]) * cg_ref[...] + cb_ref[...]).astype(BF16)
    y = _dot(c, wo_ref[:C_CONV, :]) + _dot(a_ref[0], wo_ref[C_CONV:, :]) + bo_ref[...]
    o_ref[0] = _norm(_norm(x_ref[0], ALPHA) + g1_ref[0] * y) * lg_ref[...] + lb_ref[...]


def _mix_call(glu, attn, x, g1, w, tm):
    b, n, d = x.shape
    tok = lambda bi, i: (bi, i, 0)
    row = lambda bi, i: (bi, 0, 0)
    return pl.pallas_call(
        _mix_kernel,
        grid=(b, n // tm),
        in_specs=[*_halo_specs(tm, n, C_CONV, HALO),
                  pl.BlockSpec((1, tm, HEAD_PAIRS * LANES), tok),
                  pl.BlockSpec((1, tm, d), tok),
                  pl.BlockSpec((1, 1, d), row),
                  _const_spec(w["dww"].shape), _const_spec(w["dwb"].shape),
                  _const_spec(w["cg"].shape), _const_spec(w["cb"].shape),
                  _const_spec(w["wo"].shape), _const_spec(w["bo"].shape),
                  _const_spec(w["l1g"].shape), _const_spec(w["l1b"].shape)],
        out_specs=pl.BlockSpec((1, tm, d), tok),
        out_shape=jax.ShapeDtypeStruct((b, n, d), F32),
        scratch_shapes=[pltpu.VMEM((C_CONV // LANES, tm + 2 * HALO, LANES), F32),
                        pltpu.VMEM((tm, C_CONV), F32)],
        compiler_params=_params("parallel", "parallel"),
        name="mix",
    )(glu, glu, glu, attn, x, g1, w["dww"], w["dwb"], w["cg"], w["cb"], w["wo"], w["bo"],
      w["l1g"], w["l1b"])


def _ffn_kernel(xp_ref, x_ref, xn_ref, sc_ref, sh_ref, g2_ref, wup_ref, fw_ref, fb_ref, wdn_ref,
                bdn_ref, lg_ref, lb_ref, o_ref, h_ref, u_ref, act_ref):
    tm = x_ref.shape[1]
    has_prev, has_next = _halo_masks()
    scale = 1.0 + sc_ref[0]
    shift = sh_ref[0]
    h_ref[...] = jnp.concatenate([(xp_ref[0] * scale + shift) * has_prev,
                                  x_ref[0] * scale + shift,
                                  (xn_ref[0] * scale + shift) * has_next], axis=0).astype(BF16)

    groups = FF_CHUNK // LANES

    def conv(slot, col0, u):
        outs = []
        for j in range(groups):
            u_ref[slot + j] = u[:, j * LANES:(j + 1) * LANES]
            cols = slice(col0 + j * LANES, col0 + (j + 1) * LANES)
            acc = fb_ref[:, cols]
            for kk in range(FFN_K):
                acc = acc + u_ref[slot + j, pl.ds(FFN_HALO - 1 + kk, tm), :] * fw_ref[kk:kk + 1, cols]
            outs.append(acc)
        return outs

    for c in range(D_FF // FF_CHUNK):
        g = conv(4 * (c % 2), c * FF_CHUNK, _dot(h_ref[...], wup_ref[:, c * FF_CHUNK:(c + 1) * FF_CHUNK]))
        v = conv(4 * (c % 2) + 2, D_FF + c * FF_CHUNK,
                 _dot(h_ref[...], wup_ref[:, D_FF + c * FF_CHUNK:D_FF + (c + 1) * FF_CHUNK]))
        for j in range(groups):
            cols = slice(c * FF_CHUNK + j * LANES, c * FF_CHUNK + (j + 1) * LANES)
            act_ref[:, cols] = (_silu(g[j]) * v[j]).astype(BF16)

    f = _dot(act_ref[...], wdn_ref[...]) + bdn_ref[...]
    o_ref[0] = _norm(ALPHA * x_ref[0] + g2_ref[0] * f) * lg_ref[...] + lb_ref[...]


def _ffn_call(xm, sc, sh, g2, w, tm):
    b, n, d = xm.shape
    tok = lambda bi, i: (bi, i, 0)
    row = lambda bi, i: (bi, 0, 0)
    return pl.pallas_call(
        _ffn_kernel,
        grid=(b, n // tm),
        in_specs=[*_halo_specs(tm, n, d, FFN_HALO),
                  pl.BlockSpec((1, 1, d), row), pl.BlockSpec((1, 1, d), row), pl.BlockSpec((1, 1, d), row),
                  _const_spec(w["wup"].shape), _const_spec(w["fw"].shape), _const_spec(w["fb"].shape),
                  _const_spec(w["wdn"].shape), _const_spec(w["bdn"].shape),
                  _const_spec(w["l2g"].shape), _const_spec(w["l2b"].shape)],
        out_specs=pl.BlockSpec((1, tm, d), tok),
        out_shape=jax.ShapeDtypeStruct((b, n, d), F32),
        scratch_shapes=[pltpu.VMEM((tm + 2 * FFN_HALO, d), BF16),
                        pltpu.VMEM((4 * FF_CHUNK // LANES, tm + 2 * FFN_HALO, LANES), F32),
                        pltpu.VMEM((tm, D_FF), BF16)],
        compiler_params=_params("parallel", "parallel"),
        name="ffn",
    )(xm, xm, xm, sc, sh, g2, w["wup"], w["fw"], w["fb"], w["wdn"], w["bdn"], w["l2g"], w["l2b"])


def _rope_tables(n):
    t = jnp.arange(n)
    n_freq = QK_ROPE // 4
    inv = ROPE_THETA ** (-jnp.arange(n_freq, dtype=F32) / n_freq)
    ang = jnp.concatenate([(t // GRID_W)[:, None] * inv, (t % GRID_W)[:, None] * inv], axis=-1)
    cos, sin = jnp.cos(ang), jnp.sin(ang)
    pad = jnp.zeros((n, LANES - QK_NOPE - QK_ROPE), F32)
    cos_t = jnp.concatenate([jnp.ones((n, QK_NOPE), F32), cos, cos, pad], axis=-1)
    sin_t = jnp.concatenate([jnp.zeros((n, QK_NOPE), F32), sin, sin, pad], axis=-1)
    return cos_t, sin_t


def _partner(w_rope):
    half = w_rope.shape[-1] // 2
    return jnp.concatenate([-w_rope[..., half:], w_rope[..., :half]], axis=-1)


def _layout_weights(w_in, w_uq, w_ukv):
    d = w_in.shape[0]
    tail = LANES - QK_NOPE - QK_ROPE
    w_in = w_in.astype(BF16)
    z = lambda r, c: jnp.zeros((r, c), BF16)
    kr = w_in[:, _Z_KR:_Z_KR + QK_ROPE]
    win = jnp.concatenate([w_in[:, :_Z_KR], z(d, QK_NOPE), kr, _partner(kr)], axis=-1)
    win_ctx = jnp.concatenate([w_in[:, _Z_CKV:_Z_KR], z(d, QK_NOPE), kr, z(d, tail)], axis=-1)

    wq = w_uq.reshape(Q_LORA, N_HEADS, QK_NOPE + QK_ROPE)
    wq = jnp.concatenate([wq, _partner(wq[..., QK_NOPE:])], axis=-1).reshape(Q_LORA, N_HEADS * LANES)
    wkv = w_ukv.reshape(KV_LORA, N_HEADS, QK_NOPE + V_DIM)
    wk = jnp.concatenate([wkv[..., :QK_NOPE], jnp.zeros((KV_LORA, N_HEADS, LANES - QK_NOPE), F32)],
                         axis=-1).reshape(KV_LORA, N_HEADS * LANES)
    wv = wkv[..., QK_NOPE:].reshape(KV_LORA, N_HEADS * V_DIM)
    cast = lambda a: a.astype(BF16)
    return dict(win=cast(win), win_ctx=cast(win_ctx), wq=cast(wq), wk=cast(wk), wv=cast(wv))


def kernel(x, c, ctx, c_ctx, w_ada, b_ada, w_in, conv_dw_w, conv_dw_b, conv_ln_g, conv_ln_b, q_norm_g,
           w_uq, kv_norm_g, w_ukv, w_o, b_o, ln1_g, ln1_b, w_up, ffn_dw_w, ffn_dw_b, w_down, b_down,
           ln2_g, ln2_b):
    b, n, d = x.shape
    tm = 512
    tq = 2 * ATTN_ROWS
    assert DEPTH == 1 and w_ada.shape[0] == 1
    assert n % tm == 0 and n % tq == 0 and tm % HALO == 0

    c_rows = jnp.concatenate([c, c_ctx[None, :], jnp.zeros((16 - b - 1, d), F32)], axis=0)
    mod = _mod_call(c_rows, w_ada[0], b_ada).reshape(16, 6, 1, d)
    sh1, sc1, g1, sh2, sc2, g2 = (mod[:b, j] for j in range(6))
    sh1c, sc1c = mod[b:b + 1, 0], mod[b:b + 1, 1]

    row2 = lambda a: a.reshape(1, -1)
    w = _layout_weights(w_in[0], w_uq[0], w_ukv[0])
    w.update(gq=row2(q_norm_g[0]), gkv=row2(kv_norm_g[0]))
    cos_t, sin_t = _rope_tables(n)

    glu, q, k_lat, v_lat = _proj_call(x, sc1, sh1, cos_t, sin_t, w, 2 * PROJ_ROWS)
    k_ctx, v_ctx = _proj_ctx_call(ctx, sc1c, sh1c, w)
    attn = _attn_call(q, k_lat, v_lat, k_ctx, v_ctx, tq)

    wm = dict(dww=conv_dw_w[0], dwb=row2(conv_dw_b[0]), cg=row2(conv_ln_g[0]), cb=row2(conv_ln_b[0]),
              wo=w_o[0].astype(BF16), bo=row2(b_o[0]), l1g=row2(ln1_g[0]), l1b=row2(ln1_b[0]))
    x_mid = _mix_call(glu, attn, x, g1, wm, 2 * tm)

    wf = dict(wup=w_up[0].astype(BF16), fw=ffn_dw_w[0], fb=row2(ffn_dw_b[0]), wdn=w_down[0].astype(BF16),
              bdn=row2(b_down[0]), l2g=row2(ln2_g[0]), l2b=row2(ln2_b[0]))
    return _ffn_call(x_mid, sc2, sh2, g2, wf, tm)
```

```python
import math

import jax
import jax.numpy as jnp
from jax import lax
from jax.experimental import pallas as pl
from jax.experimental.pallas import tpu as pltpu

F32 = jnp.float32
BF16 = jnp.bfloat16

D_MODEL = 1024
GRID_W = 64
C_CONV = 512
CONV_K = 31
N_HEADS = 8
QK_NOPE = 64
QK_ROPE = 32
V_DIM = 64
Q_LORA = 384
KV_LORA = 256
D_FF = 2816
FFN_K = 3
ROPE_THETA = 10000.0
LN_EPS = 1e-5
DEPTH = 1
ALPHA = (2 * DEPTH) ** 0.25
SOFTMAX_SCALE = (QK_NOPE + QK_ROPE) ** -0.5
LOG2E = math.log2(math.e)

LANES = 128
SUBLANES = 8
HEAD_PAIRS = N_HEADS // 2
V_TILE = 2 * LANES
HALO = 16
FFN_HALO = SUBLANES
FF_CHUNK = 256
KEY_CHUNK = 256
ATTN_ROWS = 512
PROJ_ROWS = 512
VMEM_LIMIT = 52 * 1024 * 1024

_Z_CQ = 2 * C_CONV
_Z_CKV = _Z_CQ + Q_LORA
_Z_KR = _Z_CKV + KV_LORA
_Z_END = _Z_KR + LANES
PARTNER_ROLL = LANES - QK_ROPE


def _sigmoid(x):
    return 0.5 * jnp.tanh(0.5 * x) + 0.5


def _silu(x):
    h = 0.5 * x
    return h * jnp.tanh(h) + h


def _norm(x, gain=1.0):
    mu = jnp.mean(x, axis=-1, keepdims=True)
    xc = x - mu
    var = jnp.mean(xc * xc, axis=-1, keepdims=True)
    return xc * (lax.rsqrt(var + LN_EPS) * gain)


def _rms(x, g):
    return x * lax.rsqrt(jnp.mean(x * x, axis=-1, keepdims=True) + LN_EPS) * g


def _dot(a, b):
    return jnp.dot(a, b, preferred_element_type=F32)


def _params(*sem):
    return pltpu.CompilerParams(dimension_semantics=sem, vmem_limit_bytes=VMEM_LIMIT)


def _const_spec(shape):
    zeros = (0,) * len(shape)
    return pl.BlockSpec(shape, lambda *_: zeros)


def _mod_kernel(c_ref, w_ref, b_ref, o_ref):
    c = c_ref[...]
    a = _silu(c)
    o_ref[...] = jnp.dot(a, w_ref[...], precision=lax.Precision.HIGHEST,
                         preferred_element_type=F32) + b_ref[...]


def _mod_call(c_rows, w_ada, b_ada):
    rows, d = c_rows.shape
    n = w_ada.shape[1]
    bn = 1536
    return pl.pallas_call(
        _mod_kernel,
        grid=(n // bn,),
        in_specs=[pl.BlockSpec((rows, d), lambda j: (0, 0)),
                  pl.BlockSpec((d, bn), lambda j: (0, j)),
                  pl.BlockSpec((1, bn), lambda j: (0, j))],
        out_specs=pl.BlockSpec((rows, bn), lambda j: (0, j)),
        out_shape=jax.ShapeDtypeStruct((rows, n), F32),
        compiler_params=_params("arbitrary"),
        name="mod",
    )(c_rows, w_ada, b_ada)


def _store_kv(ckv, kr128, gkv_ref, wk_ref, wv_ref, k_ref, v_ref, rows):
    ckvn = _rms(ckv, gkv_ref[...]).astype(BF16)
    kk = _dot(ckvn, wk_ref[...])
    lane = lax.broadcasted_iota(jnp.int32, kr128.shape, 1)
    for h in range(N_HEADS):
        pair = kk[:, (h // 2) * LANES:(h // 2 + 1) * LANES]
        if h % 2:
            pair = pltpu.roll(pair, QK_NOPE, 1)
        k_ref[0, h, rows] = jnp.where(lane < QK_NOPE, pair, kr128).astype(BF16)
    vv = _dot(ckvn, wv_ref[...])
    one_col = (lax.broadcasted_iota(jnp.int32, (vv.shape[0], LANES), 1) == 0).astype(BF16)
    for p in range(HEAD_PAIRS):
        v_ref[0, p, rows, :LANES] = vv[:, p * LANES:(p + 1) * LANES].astype(BF16)
        v_ref[0, p, rows, LANES:] = one_col


def _rotary(t, cos, sin):
    return t * cos + pltpu.roll(t, PARTNER_ROLL, 1) * sin


def _proj_kernel(x_ref, sc_ref, sh_ref, cos_ref, sin_ref, win_ref, gq_ref, wq_ref,
                 gkv_ref, wk_ref, wv_ref, glu_ref, q_ref, k_ref, v_ref):
    for r0 in range(0, x_ref.shape[1], PROJ_ROWS):
        rows = slice(r0, r0 + PROJ_ROWS)
        h = _norm(x_ref[0, rows]) * (1.0 + sc_ref[0]) + sh_ref[0]
        z = _dot(h.astype(BF16), win_ref[...])
        glu_ref[0, rows] = z[:, :C_CONV] * _sigmoid(z[:, C_CONV:_Z_CQ])
        cos = cos_ref[rows]
        sin = sin_ref[rows]
        cqn = _rms(z[:, _Z_CQ:_Z_CKV], gq_ref[...]).astype(BF16)
        qa = _dot(cqn, wq_ref[...])
        for hd in range(N_HEADS):
            q = _rotary(qa[:, hd * LANES:(hd + 1) * LANES], cos, sin) * (SOFTMAX_SCALE * LOG2E)
            q_ref[0, hd, rows] = q.astype(BF16)
        kr128 = _rotary(z[:, _Z_KR:_Z_END], cos, sin)
        _store_kv(z[:, _Z_CKV:_Z_KR], kr128, gkv_ref, wk_ref, wv_ref, k_ref, v_ref, rows)


def _proj_ctx_kernel(x_ref, sc_ref, sh_ref, win_ref, gkv_ref, wk_ref, wv_ref, k_ref, v_ref):
    h = _norm(x_ref[0]) * (1.0 + sc_ref[0]) + sh_ref[0]
    z = _dot(h.astype(BF16), win_ref[...])
    _store_kv(z[:, :KV_LORA], z[:, KV_LORA:], gkv_ref, wk_ref, wv_ref, k_ref, v_ref, slice(None))


def _proj_call(x, sc, sh, cos_t, sin_t, w, tm):
    b, n, d = x.shape
    nt = n // tm
    tok = lambda bi, i: (bi, i, 0)
    row = lambda bi, i: (bi, 0, 0)
    tab = lambda bi, i: (i, 0)
    hd4 = lambda bi, i: (bi, 0, i, 0)
    return pl.pallas_call(
        _proj_kernel,
        grid=(b, nt),
        in_specs=[pl.BlockSpec((1, tm, d), tok),
                  pl.BlockSpec((1, 1, d), row), pl.BlockSpec((1, 1, d), row),
                  pl.BlockSpec((tm, LANES), tab), pl.BlockSpec((tm, LANES), tab),
                  _const_spec(w["win"].shape), _const_spec(w["gq"].shape), _const_spec(w["wq"].shape),
                  _const_spec(w["gkv"].shape), _const_spec(w["wk"].shape), _const_spec(w["wv"].shape)],
        out_specs=[pl.BlockSpec((1, tm, C_CONV), tok),
                   pl.BlockSpec((1, N_HEADS, tm, LANES), hd4),
                   pl.BlockSpec((1, N_HEADS, tm, LANES), hd4),
                   pl.BlockSpec((1, HEAD_PAIRS, tm, V_TILE), hd4)],
        out_shape=[jax.ShapeDtypeStruct((b, n, C_CONV), F32),
                   jax.ShapeDtypeStruct((b, N_HEADS, n, LANES), BF16),
                   jax.ShapeDtypeStruct((b, N_HEADS, n, LANES), BF16),
                   jax.ShapeDtypeStruct((b, HEAD_PAIRS, n, V_TILE), BF16)],
        compiler_params=_params("parallel", "parallel"),
        name="proj",
    )(x, sc, sh, cos_t, sin_t, w["win"], w["gq"], w["wq"], w["gkv"], w["wk"], w["wv"])


def _proj_ctx_call(ctx, sc, sh, w):
    b, n_ctx, d = ctx.shape
    row = lambda bi: (0, 0, 0)
    kv = lambda bi: (bi, 0, 0, 0)
    return pl.pallas_call(
        _proj_ctx_kernel,
        grid=(b,),
        in_specs=[pl.BlockSpec((1, n_ctx, d), lambda bi: (bi, 0, 0)),
                  pl.BlockSpec((1, 1, d), row), pl.BlockSpec((1, 1, d), row),
                  _const_spec(w["win_ctx"].shape), _const_spec(w["gkv"].shape),
                  _const_spec(w["wk"].shape), _const_spec(w["wv"].shape)],
        out_specs=[pl.BlockSpec((1, N_HEADS, n_ctx, LANES), kv),
                   pl.BlockSpec((1, HEAD_PAIRS, n_ctx, V_TILE), kv)],
        out_shape=[jax.ShapeDtypeStruct((b, N_HEADS, n_ctx, LANES), BF16),
                   jax.ShapeDtypeStruct((b, HEAD_PAIRS, n_ctx, V_TILE), BF16)],
        compiler_params=_params("parallel"),
        name="proj_ctx",
    )(ctx, sc, sh, w["win_ctx"], w["gkv"], w["wk"], w["wv"])


def _attn_kernel(q_ref, k_ref, v_ref, kc_ref, vc_ref, o_ref):
    chunks = [(k_ref, v_ref, c0) for c0 in range(0, k_ref.shape[2], KEY_CHUNK)]
    chunks += [(kc_ref, vc_ref, c0) for c0 in range(0, kc_ref.shape[2], KEY_CHUNK)]
    n_chunks = len(chunks)
    n_sub = q_ref.shape[2] // ATTN_ROWS

    def scores(unit, c):
        sub, j = unit
        keys, _, c0 = chunks[c]
        return lax.dot_general(q_ref[0, j, sub * ATTN_ROWS:(sub + 1) * ATTN_ROWS],
                               keys[0, j, c0:c0 + KEY_CHUNK],
                               (((1,), (1,)), ((), ())), preferred_element_type=F32)

    def row_max(s):
        m = s[0]
        for sc in s[1:]:
            m = jnp.maximum(m, sc)
        return jnp.max(m, axis=-1, keepdims=True)

    def weighted_values(s, m, c, acc):
        _, values, c0 = chunks[c]
        pv = _dot(jnp.exp2(s[c] - m).astype(BF16), values[0, 0, c0:c0 + KEY_CHUNK])
        return pv if acc is None else acc + pv

    units = [(sub, j) for sub in range(n_sub) for j in range(2)]
    s_cur = [scores(units[0], c) for c in range(n_chunks)]
    outs = []
    for u in range(len(units)):
        m_cur, acc, s_nxt = row_max(s_cur), None, []
        for c in range(n_chunks):
            if u + 1 < len(units):
                s_nxt.append(scores(units[u + 1], c))
            acc = weighted_values(s_cur, m_cur, c, acc)
        outs.append(acc[:, :LANES] * (1.0 / acc[:, LANES:LANES + 1]))
        s_cur = s_nxt
    lane = lax.broadcasted_iota(jnp.int32, outs[0].shape, 1)
    for sub in range(n_sub):
        o_ref[0, sub * ATTN_ROWS:(sub + 1) * ATTN_ROWS] = jnp.where(
            lane < V_DIM, outs[2 * sub], outs[2 * sub + 1]).astype(BF16)


def _attn_call(q, k, v, kc, vc, tq):
    b, _, n, _ = q.shape
    nk, nc = k.shape[2], kc.shape[2]
    assert nk % KEY_CHUNK == 0 and nc % KEY_CHUNK == 0
    return pl.pallas_call(
        _attn_kernel,
        grid=(b, HEAD_PAIRS, n // tq),
        in_specs=[pl.BlockSpec((1, 2, tq, LANES), lambda bi, p, i: (bi, p, i, 0)),
                  pl.BlockSpec((1, 2, nk, LANES), lambda bi, p, i: (bi, p, 0, 0)),
                  pl.BlockSpec((1, 1, nk, V_TILE), lambda bi, p, i: (bi, p, 0, 0)),
                  pl.BlockSpec((1, 2, nc, LANES), lambda bi, p, i: (bi, p, 0, 0)),
                  pl.BlockSpec((1, 1, nc, V_TILE), lambda bi, p, i: (bi, p, 0, 0))],
        out_specs=pl.BlockSpec((1, tq, LANES), lambda bi, p, i: (bi, i, p)),
        out_shape=jax.ShapeDtypeStruct((b, n, HEAD_PAIRS * LANES), BF16),
        compiler_params=_params("parallel", "parallel", "arbitrary"),
        name="attn",
    )(q, k, v, kc, vc)


def _halo_specs(tm, n, width, halo):
    per = tm // halo
    last = n // halo - 1
    return (pl.BlockSpec((1, halo, width), lambda bi, i: (bi, jnp.maximum(i * per - 1, 0), 0)),
            pl.BlockSpec((1, tm, width), lambda bi, i: (bi, i, 0)),
            pl.BlockSpec((1, halo, width), lambda bi, i: (bi, jnp.minimum((i + 1) * per, last), 0)))


def _halo_masks():
    i = pl.program_id(1)
    has_prev = (i > 0).astype(F32)
    has_next = (i < pl.num_programs(1) - 1).astype(F32)
    return has_prev, has_next


MIX_ROWS = 256
CONV_ROWS = 64


def _mix_kernel(gp_ref, g_ref, gn_ref, a_ref, x_ref, g1_ref, dww_ref, dwb_ref, cg_ref, cb_ref,
                wo_ref, bo_ref, lg_ref, lb_ref, o_ref, win_ref, conv_ref):
    tm = g_ref.shape[1]
    has_prev, has_next = _halo_masks()
    for j in range(C_CONV // LANES):
        cols = slice(j * LANES, (j + 1) * LANES)
        win_ref[j, 0:HALO] = gp_ref[0, :, cols] * has_prev
        win_ref[j, HALO:HALO + tm] = g_ref[0, :, cols]
        win_ref[j, HALO + tm:] = gn_ref[0, :, cols] * has_next
    half = CONV_K // 2

    def rows(c, carry):
        base = pl.multiple_of(c * CONV_ROWS, CONV_ROWS)
        for j in range(C_CONV // LANES):
            cols = slice(j * LANES, (j + 1) * LANES)
            acc = jnp.zeros((CONV_ROWS, LANES), F32) + dwb_ref[:, cols]
            for kk in range(CONV_K):
                acc = acc + (win_ref[j, pl.ds(base + (HALO - half + kk), CONV_ROWS), :]
                             * dww_ref[kk:kk + 1, cols])
            conv_ref[pl.ds(base, CONV_ROWS), cols] = acc
        return carry

    lax.fori_loop(0, tm // CONV_ROWS, rows, 0)
    for r0 in range(0, tm, MIX_ROWS):
        rows = slice(r0, r0 + MIX_ROWS)
        c = _silu(_norm(conv_ref[rows, :]) * cg_ref[...] + cb_ref[...]).astype(BF16)
        y = _dot(c, wo_ref[:C_CONV, :]) + _dot(a_ref[0, rows], wo_ref[C_CONV:, :]) + bo_ref[...]
        o_ref[0, rows] = _norm(_norm(x_ref[0, rows], ALPHA) + g1_ref[0] * y) * lg_ref[...] + lb_ref[...]


def _mix_call(glu, attn, x, g1, w, tm):
    b, n, d = x.shape
    tok = lambda bi, i: (bi, i, 0)
    row = lambda bi, i: (bi, 0, 0)
    return pl.pallas_call(
        _mix_kernel,
        grid=(b, n // tm),
        in_specs=[*_halo_specs(tm, n, C_CONV, HALO),
                  pl.BlockSpec((1, tm, HEAD_PAIRS * LANES), tok),
                  pl.BlockSpec((1, tm, d), tok),
                  pl.BlockSpec((1, 1, d), row),
                  _const_spec(w["dww"].shape), _const_spec(w["dwb"].shape),
                  _const_spec(w["cg"].shape), _const_spec(w["cb"].shape),
                  _const_spec(w["wo"].shape), _const_spec(w["bo"].shape),
                  _const_spec(w["l1g"].shape), _const_spec(w["l1b"].shape)],
        out_specs=pl.BlockSpec((1, tm, d), tok),
        out_shape=jax.ShapeDtypeStruct((b, n, d), F32),
        scratch_shapes=[pltpu.VMEM((C_CONV // LANES, tm + 2 * HALO, LANES), F32),
                        pltpu.VMEM((tm, C_CONV), F32)],
        compiler_params=_params("parallel", "parallel"),
        name="mix",
    )(glu, glu, glu, attn, x, g1, w["dww"], w["dwb"], w["cg"], w["cb"], w["wo"], w["bo"],
      w["l1g"], w["l1b"])


def _ffn_kernel(xp_ref, x_ref, xn_ref, sc_ref, sh_ref, g2_ref, wup_ref, fw_ref, fb_ref, wdn_ref,
                bdn_ref, lg_ref, lb_ref, o_ref, h_ref, u_ref, act_ref):
    tm = x_ref.shape[1]
    has_prev, has_next = _halo_masks()
    scale = 1.0 + sc_ref[0]
    shift = sh_ref[0]
    h_ref[...] = jnp.concatenate([(xp_ref[0] * scale + shift) * has_prev,
                                  x_ref[0] * scale + shift,
                                  (xn_ref[0] * scale + shift) * has_next], axis=0).astype(BF16)

    groups = FF_CHUNK // LANES

    def conv(slot, col0, u):
        outs = []
        for j in range(groups):
            u_ref[slot + j] = u[:, j * LANES:(j + 1) * LANES]
            cols = slice(col0 + j * LANES, col0 + (j + 1) * LANES)
            acc = fb_ref[:, cols]
            for kk in range(FFN_K):
                acc = acc + u_ref[slot + j, pl.ds(FFN_HALO - 1 + kk, tm), :] * fw_ref[kk:kk + 1, cols]
            outs.append(acc)
        return outs

    for c in range(D_FF // FF_CHUNK):
        g = conv(4 * (c % 2), c * FF_CHUNK, _dot(h_ref[...], wup_ref[:, c * FF_CHUNK:(c + 1) * FF_CHUNK]))
        v = conv(4 * (c % 2) + 2, D_FF + c * FF_CHUNK,
                 _dot(h_ref[...], wup_ref[:, D_FF + c * FF_CHUNK:D_FF + (c + 1) * FF_CHUNK]))
        for j in range(groups):
            cols = slice(c * FF_CHUNK + j * LANES, c * FF_CHUNK + (j + 1) * LANES)
            act_ref[:, cols] = (_silu(g[j]) * v[j]).astype(BF16)

    for r0 in range(0, tm, tm // 2):
        rows = slice(r0, r0 + tm // 2)
        f = _dot(act_ref[rows, :], wdn_ref[...]) + bdn_ref[...]
        o_ref[0, rows] = _norm(ALPHA * x_ref[0, rows] + g2_ref[0] * f) * lg_ref[...] + lb_ref[...]


def _ffn_call(xm, sc, sh, g2, w, tm):
    b, n, d = xm.shape
    tok = lambda bi, i: (bi, i, 0)
    row = lambda bi, i: (bi, 0, 0)
    return pl.pallas_call(
        _ffn_kernel,
        grid=(b, n // tm),
        in_specs=[*_halo_specs(tm, n, d, FFN_HALO),
                  pl.BlockSpec((1, 1, d), row), pl.BlockSpec((1, 1, d), row), pl.BlockSpec((1, 1, d), row),
                  _const_spec(w["wup"].shape), _const_spec(w["fw"].shape), _const_spec(w["fb"].shape),
                  _const_spec(w["wdn"].shape), _const_spec(w["bdn"].shape),
                  _const_spec(w["l2g"].shape), _const_spec(w["l2b"].shape)],
        out_specs=pl.BlockSpec((1, tm, d), tok),
        out_shape=jax.ShapeDtypeStruct((b, n, d), F32),
        scratch_shapes=[pltpu.VMEM((tm + 2 * FFN_HALO, d), BF16),
                        pltpu.VMEM((4 * FF_CHUNK // LANES, tm + 2 * FFN_HALO, LANES), F32),
                        pltpu.VMEM((tm, D_FF), BF16)],
        compiler_params=_params("parallel", "parallel"),
        name="ffn",
    )(xm, xm, xm, sc, sh, g2, w["wup"], w["fw"], w["fb"], w["wdn"], w["bdn"], w["l2g"], w["l2b"])


def _rope_tables(n):
    t = jnp.arange(n)
    n_freq = QK_ROPE // 4
    inv = ROPE_THETA ** (-jnp.arange(n_freq, dtype=F32) / n_freq)
    ang = jnp.concatenate([(t // GRID_W)[:, None] * inv, (t % GRID_W)[:, None] * inv], axis=-1)
    cos, sin = jnp.cos(ang), jnp.sin(ang)
    pad = jnp.zeros((n, LANES - QK_NOPE - QK_ROPE), F32)
    cos_t = jnp.concatenate([jnp.ones((n, QK_NOPE), F32), cos, cos, pad], axis=-1)
    sin_t = jnp.concatenate([jnp.zeros((n, QK_NOPE), F32), sin, sin, pad], axis=-1)
    return cos_t, sin_t


def _partner(w_rope):
    half = w_rope.shape[-1] // 2
    return jnp.concatenate([-w_rope[..., half:], w_rope[..., :half]], axis=-1)


def _layout_weights(w_in, w_uq, w_ukv):
    d = w_in.shape[0]
    tail = LANES - QK_NOPE - QK_ROPE
    w_in = w_in.astype(BF16)
    z = lambda r, c: jnp.zeros((r, c), BF16)
    kr = w_in[:, _Z_KR:_Z_KR + QK_ROPE]
    win = jnp.concatenate([w_in[:, :_Z_KR], z(d, QK_NOPE), kr, _partner(kr)], axis=-1)
    win_ctx = jnp.concatenate([w_in[:, _Z_CKV:_Z_KR], z(d, QK_NOPE), kr, z(d, tail)], axis=-1)

    wq = w_uq.reshape(Q_LORA, N_HEADS, QK_NOPE + QK_ROPE)
    wq = jnp.concatenate([wq, _partner(wq[..., QK_NOPE:])], axis=-1).reshape(Q_LORA, N_HEADS * LANES)
    wkv = w_ukv.reshape(KV_LORA, N_HEADS, QK_NOPE + V_DIM)
    wk = wkv[..., :QK_NOPE].reshape(KV_LORA, N_HEADS * QK_NOPE)
    wv = wkv[..., QK_NOPE:].reshape(KV_LORA, N_HEADS * V_DIM)
    cast = lambda a: a.astype(BF16)
    return dict(win=cast(win), win_ctx=cast(win_ctx), wq=cast(wq), wk=cast(wk), wv=cast(wv))


def kernel(x, c, ctx, c_ctx, w_ada, b_ada, w_in, conv_dw_w, conv_dw_b, conv_ln_g, conv_ln_b, q_norm_g,
           w_uq, kv_norm_g, w_ukv, w_o, b_o, ln1_g, ln1_b, w_up, ffn_dw_w, ffn_dw_b, w_down, b_down,
           ln2_g, ln2_b):
    b, n, d = x.shape
    tm = 512
    tq = 2 * ATTN_ROWS
    assert DEPTH == 1 and w_ada.shape[0] == 1
    assert n % tm == 0 and n % tq == 0 and tm % HALO == 0

    c_rows = jnp.concatenate([c, c_ctx[None, :], jnp.zeros((16 - b - 1, d), F32)], axis=0)
    mod = _mod_call(c_rows, w_ada[0], b_ada).reshape(16, 6, 1, d)
    sh1, sc1, g1, sh2, sc2, g2 = (mod[:b, j] for j in range(6))
    sh1c, sc1c = mod[b:b + 1, 0], mod[b:b + 1, 1]

    row2 = lambda a: a.reshape(1, -1)
    w = _layout_weights(w_in[0], w_uq[0], w_ukv[0])
    w.update(gq=row2(q_norm_g[0]), gkv=row2(kv_norm_g[0]))
    cos_t, sin_t = _rope_tables(n)

    glu, q, k_lat, v_lat = _proj_call(x, sc1, sh1, cos_t, sin_t, w, 2 * PROJ_ROWS)
    k_ctx, v_ctx = _proj_ctx_call(ctx, sc1c, sh1c, w)
    attn = _attn_call(q, k_lat, v_lat, k_ctx, v_ctx, tq)

    wm = dict(dww=conv_dw_w[0], dwb=row2(conv_dw_b[0]), cg=row2(conv_ln_g[0]), cb=row2(conv_ln_b[0]),
              wo=w_o[0].astype(BF16), bo=row2(b_o[0]), l1g=row2(ln1_g[0]), l1b=row2(ln1_b[0]))
    x_mid = _mix_call(glu, attn, x, g1, wm, 2 * tm)

    wf = dict(wup=w_up[0].astype(BF16), fw=ffn_dw_w[0], fb=row2(ffn_dw_b[0]), wdn=w_down[0].astype(BF16),
              bdn=row2(b_down[0]), l2g=row2(ln2_g[0]), l2b=row2(ln2_b[0]))
    return _ffn_call(x_mid, sc2, sh2, g2, wf, tm)
```

```python
import math

import jax
import jax.numpy as jnp
from jax import lax
from jax.experimental import pallas as pl
from jax.experimental.pallas import tpu as pltpu

F32 = jnp.float32
BF16 = jnp.bfloat16

D_MODEL = 1024
GRID_W = 64
C_CONV = 512
CONV_K = 31
N_HEADS = 8
QK_NOPE = 64
QK_ROPE = 32
V_DIM = 64
Q_LORA = 384
KV_LORA = 256
D_FF = 2816
FFN_K = 3
ROPE_THETA = 10000.0
LN_EPS = 1e-5
DEPTH = 1
ALPHA = (2 * DEPTH) ** 0.25
SOFTMAX_SCALE = (QK_NOPE + QK_ROPE) ** -0.5
LOG2E = math.log2(math.e)

LANES = 128
SUBLANES = 8
HEAD_PAIRS = N_HEADS // 2
V_TILE = 2 * LANES
HALO = 16
FFN_HALO = SUBLANES
FF_CHUNK = 256
KEY_CHUNK = 256
ATTN_ROWS = 512
PROJ_ROWS = 512
VMEM_LIMIT = 52 * 1024 * 1024

_Z_CQ = 2 * C_CONV
_Z_CKV = _Z_CQ + Q_LORA
_Z_KR = _Z_CKV + KV_LORA
_Z_END = _Z_KR + LANES
PARTNER_ROLL = LANES - QK_ROPE


def _sigmoid(x):
    return 0.5 * jnp.tanh(0.5 * x) + 0.5


def _silu(x):
    h = 0.5 * x
    return h * jnp.tanh(h) + h


def _norm(x, gain=1.0):
    mu = jnp.mean(x, axis=-1, keepdims=True)
    xc = x - mu
    var = jnp.mean(xc * xc, axis=-1, keepdims=True)
    return xc * (lax.rsqrt(var + LN_EPS) * gain)


def _rms(x, g):
    return x * lax.rsqrt(jnp.mean(x * x, axis=-1, keepdims=True) + LN_EPS) * g


def _dot(a, b):
    return jnp.dot(a, b, preferred_element_type=F32)


def _params(*sem):
    return pltpu.CompilerParams(dimension_semantics=sem, vmem_limit_bytes=VMEM_LIMIT)


def _const_spec(shape):
    zeros = (0,) * len(shape)
    return pl.BlockSpec(shape, lambda *_: zeros)


def _mod_kernel(c_ref, w_ref, b_ref, o_ref):
    c = c_ref[...]
    a = _silu(c)
    o_ref[...] = jnp.dot(a, w_ref[...], precision=lax.Precision.HIGHEST,
                         preferred_element_type=F32) + b_ref[...]


def _mod_call(c_rows, w_ada, b_ada):
    rows, d = c_rows.shape
    n = w_ada.shape[1]
    bn = 1536
    return pl.pallas_call(
        _mod_kernel,
        grid=(n // bn,),
        in_specs=[pl.BlockSpec((rows, d), lambda j: (0, 0)),
                  pl.BlockSpec((d, bn), lambda j: (0, j)),
                  pl.BlockSpec((1, bn), lambda j: (0, j))],
        out_specs=pl.BlockSpec((rows, bn), lambda j: (0, j)),
        out_shape=jax.ShapeDtypeStruct((rows, n), F32),
        compiler_params=_params("arbitrary"),
        name="mod",
    )(c_rows, w_ada, b_ada)


def _store_kv(ckv, kr128, gkv_ref, wk_ref, wv_ref, k_ref, v_ref, rows):
    ckvn = _rms(ckv, gkv_ref[...]).astype(BF16)
    kk = _dot(ckvn, wk_ref[...])
    lane = lax.broadcasted_iota(jnp.int32, kr128.shape, 1)
    for h in range(N_HEADS):
        pair = kk[:, (h // 2) * LANES:(h // 2 + 1) * LANES]
        if h % 2:
            pair = pltpu.roll(pair, QK_NOPE, 1)
        k_ref[0, h, rows] = jnp.where(lane < QK_NOPE, pair, kr128).astype(BF16)
    vv = _dot(ckvn, wv_ref[...])
    one_col = (lax.broadcasted_iota(jnp.int32, (vv.shape[0], LANES), 1) == 0).astype(BF16)
    for p in range(HEAD_PAIRS):
        v_ref[0, p, rows, :LANES] = vv[:, p * LANES:(p + 1) * LANES].astype(BF16)
        v_ref[0, p, rows, LANES:] = one_col


def _rotary(t, cos, sin):
    return t * cos + pltpu.roll(t, PARTNER_ROLL, 1) * sin


def _proj_kernel(x_ref, sc_ref, sh_ref, cos_ref, sin_ref, win_ref, gq_ref, wq_ref,
                 gkv_ref, wk_ref, wv_ref, glu_ref, q_ref, k_ref, v_ref):
    for r0 in range(0, x_ref.shape[1], PROJ_ROWS):
        rows = slice(r0, r0 + PROJ_ROWS)
        h = _norm(x_ref[0, rows]) * (1.0 + sc_ref[0]) + sh_ref[0]
        z = _dot(h.astype(BF16), win_ref[...])
        glu_ref[0, rows] = z[:, :C_CONV] * _sigmoid(z[:, C_CONV:_Z_CQ])
        cos = cos_ref[rows]
        sin = sin_ref[rows]
        cqn = _rms(z[:, _Z_CQ:_Z_CKV], gq_ref[...]).astype(BF16)
        qa = _dot(cqn, wq_ref[...])
        for hd in range(N_HEADS):
            q = _rotary(qa[:, hd * LANES:(hd + 1) * LANES], cos, sin) * (SOFTMAX_SCALE * LOG2E)
            q_ref[0, hd, rows] = q.astype(BF16)
        kr128 = _rotary(z[:, _Z_KR:_Z_END], cos, sin)
        _store_kv(z[:, _Z_CKV:_Z_KR], kr128, gkv_ref, wk_ref, wv_ref, k_ref, v_ref, rows)


def _proj_ctx_kernel(x_ref, sc_ref, sh_ref, win_ref, gkv_ref, wk_ref, wv_ref, k_ref, v_ref):
    h = _norm(x_ref[0]) * (1.0 + sc_ref[0]) + sh_ref[0]
    z = _dot(h.astype(BF16), win_ref[...])
    _store_kv(z[:, :KV_LORA], z[:, KV_LORA:], gkv_ref, wk_ref, wv_ref, k_ref, v_ref, slice(None))


def _proj_call(x, sc, sh, cos_t, sin_t, w, tm):
    b, n, d = x.shape
    nt = n // tm
    tok = lambda bi, i: (bi, i, 0)
    row = lambda bi, i: (bi, 0, 0)
    tab = lambda bi, i: (i, 0)
    hd4 = lambda bi, i: (bi, 0, i, 0)
    return pl.pallas_call(
        _proj_kernel,
        grid=(b, nt),
        in_specs=[pl.BlockSpec((1, tm, d), tok),
                  pl.BlockSpec((1, 1, d), row), pl.BlockSpec((1, 1, d), row),
                  pl.BlockSpec((tm, LANES), tab), pl.BlockSpec((tm, LANES), tab),
                  _const_spec(w["win"].shape), _const_spec(w["gq"].shape), _const_spec(w["wq"].shape),
                  _const_spec(w["gkv"].shape), _const_spec(w["wk"].shape), _const_spec(w["wv"].shape)],
        out_specs=[pl.BlockSpec((1, tm, C_CONV), tok),
                   pl.BlockSpec((1, N_HEADS, tm, LANES), hd4),
                   pl.BlockSpec((1, N_HEADS, tm, LANES), hd4),
                   pl.BlockSpec((1, HEAD_PAIRS, tm, V_TILE), hd4)],
        out_shape=[jax.ShapeDtypeStruct((b, n, C_CONV), F32),
                   jax.ShapeDtypeStruct((b, N_HEADS, n, LANES), BF16),
                   jax.ShapeDtypeStruct((b, N_HEADS, n, LANES), BF16),
                   jax.ShapeDtypeStruct((b, HEAD_PAIRS, n, V_TILE), BF16)],
        compiler_params=_params("parallel", "parallel"),
        name="proj",
    )(x, sc, sh, cos_t, sin_t, w["win"], w["gq"], w["wq"], w["gkv"], w["wk"], w["wv"])


def _proj_ctx_call(ctx, sc, sh, w):
    b, n_ctx, d = ctx.shape
    row = lambda bi: (0, 0, 0)
    kv = lambda bi: (bi, 0, 0, 0)
    return pl.pallas_call(
        _proj_ctx_kernel,
        grid=(b,),
        in_specs=[pl.BlockSpec((1, n_ctx, d), lambda bi: (bi, 0, 0)),
                  pl.BlockSpec((1, 1, d), row), pl.BlockSpec((1, 1, d), row),
                  _const_spec(w["win_ctx"].shape), _const_spec(w["gkv"].shape),
                  _const_spec(w["wk"].shape), _const_spec(w["wv"].shape)],
        out_specs=[pl.BlockSpec((1, N_HEADS, n_ctx, LANES), kv),
                   pl.BlockSpec((1, HEAD_PAIRS, n_ctx, V_TILE), kv)],
        out_shape=[jax.ShapeDtypeStruct((b, N_HEADS, n_ctx, LANES), BF16),
                   jax.ShapeDtypeStruct((b, HEAD_PAIRS, n_ctx, V_TILE), BF16)],
        compiler_params=_params("parallel"),
        name="proj_ctx",
    )(ctx, sc, sh, w["win_ctx"], w["gkv"], w["wk"], w["wv"])


def _attn_kernel(q_ref, k_ref, v_ref, kc_ref, vc_ref, o_ref):
    chunks = [(k_ref, v_ref, c0) for c0 in range(0, k_ref.shape[2], KEY_CHUNK)]
    chunks += [(kc_ref, vc_ref, c0) for c0 in range(0, kc_ref.shape[2], KEY_CHUNK)]
    n_chunks = len(chunks)
    n_sub = q_ref.shape[2] // ATTN_ROWS

    def scores(unit, c):
        sub, j = unit
        keys, _, c0 = chunks[c]
        return lax.dot_general(q_ref[0, j, sub * ATTN_ROWS:(sub + 1) * ATTN_ROWS],
                               keys[0, j, c0:c0 + KEY_CHUNK],
                               (((1,), (1,)), ((), ())), preferred_element_type=F32)

    def row_max(s):
        m = s[0]
        for sc in s[1:]:
            m = jnp.maximum(m, sc)
        return jnp.max(m, axis=-1, keepdims=True)

    def weighted_values(s, m, c, acc):
        _, values, c0 = chunks[c]
        pv = _dot(jnp.exp2(s[c] - m).astype(BF16), values[0, 0, c0:c0 + KEY_CHUNK])
        return pv if acc is None else acc + pv

    units = [(sub, j) for sub in range(n_sub) for j in range(2)]
    s_cur = [scores(units[0], c) for c in range(n_chunks)]
    outs = []
    for u in range(len(units)):
        m_cur, acc, s_nxt = row_max(s_cur), None, []
        for c in range(n_chunks):
            if u + 1 < len(units):
                s_nxt.append(scores(units[u + 1], c))
            acc = weighted_values(s_cur, m_cur, c, acc)
        outs.append(acc[:, :LANES] * (1.0 / acc[:, LANES:LANES + 1]))
        s_cur = s_nxt
    lane = lax.broadcasted_iota(jnp.int32, outs[0].shape, 1)
    for sub in range(n_sub):
        o_ref[0, sub * ATTN_ROWS:(sub + 1) * ATTN_ROWS] = jnp.where(
            lane < V_DIM, outs[2 * sub], outs[2 * sub + 1]).astype(BF16)


def _attn_call(q, k, v, kc, vc, tq):
    b, _, n, _ = q.shape
    nk, nc = k.shape[2], kc.shape[2]
    assert nk % KEY_CHUNK == 0 and nc % KEY_CHUNK == 0
    return pl.pallas_call(
        _attn_kernel,
        grid=(b, HEAD_PAIRS, n // tq),
        in_specs=[pl.BlockSpec((1, 2, tq, LANES), lambda bi, p, i: (bi, p, i, 0)),
                  pl.BlockSpec((1, 2, nk, LANES), lambda bi, p, i: (bi, p, 0, 0)),
                  pl.BlockSpec((1, 1, nk, V_TILE), lambda bi, p, i: (bi, p, 0, 0)),
                  pl.BlockSpec((1, 2, nc, LANES), lambda bi, p, i: (bi, p, 0, 0)),
                  pl.BlockSpec((1, 1, nc, V_TILE), lambda bi, p, i: (bi, p, 0, 0))],
        out_specs=pl.BlockSpec((1, tq, LANES), lambda bi, p, i: (bi, i, p)),
        out_shape=jax.ShapeDtypeStruct((b, n, HEAD_PAIRS * LANES), BF16),
        compiler_params=_params("parallel", "parallel", "arbitrary"),
        name="attn",
    )(q, k, v, kc, vc)


def _halo_specs(tm, n, width, halo):
    per = tm // halo
    last = n // halo - 1
    return (pl.BlockSpec((1, halo, width), lambda bi, i: (bi, jnp.maximum(i * per - 1, 0), 0)),
            pl.BlockSpec((1, tm, width), lambda bi, i: (bi, i, 0)),
            pl.BlockSpec((1, halo, width), lambda bi, i: (bi, jnp.minimum((i + 1) * per, last), 0)))


def _halo_masks():
    i = pl.program_id(1)
    has_prev = (i > 0).astype(F32)
    has_next = (i < pl.num_programs(1) - 1).astype(F32)
    return has_prev, has_next


CONV_ROWS = 64


def _mix_kernel(gp_ref, g_ref, gn_ref, a_ref, x_ref, g1_ref, dww_ref, dwb_ref, cg_ref, cb_ref,
                wo_ref, bo_ref, lg_ref, lb_ref, o_ref, win_ref, conv_ref):
    tm = g_ref.shape[1]
    has_prev, has_next = _halo_masks()
    for j in range(C_CONV // LANES):
        cols = slice(j * LANES, (j + 1) * LANES)
        win_ref[j, 0:HALO] = gp_ref[0, :, cols] * has_prev
        win_ref[j, HALO:HALO + tm] = g_ref[0, :, cols]
        win_ref[j, HALO + tm:] = gn_ref[0, :, cols] * has_next
    half = CONV_K // 2

    def rows(c, carry):
        base = pl.multiple_of(c * CONV_ROWS, CONV_ROWS)
        for j in range(C_CONV // LANES):
            cols = slice(j * LANES, (j + 1) * LANES)
            acc = jnp.zeros((CONV_ROWS, LANES), F32) + dwb_ref[:, cols]
            for kk in range(CONV_K):
                acc = acc + (win_ref[j, pl.ds(base + (HALO - half + kk), CONV_ROWS), :]
                             * dww_ref[kk:kk + 1, cols])
            conv_ref[pl.ds(base, CONV_ROWS), cols] = acc
        return carry

    lax.fori_loop(0, tm // CONV_ROWS, rows, 0)
    c = _silu(_norm(conv_ref[...]) * cg_ref[...] + cb_ref[...]).astype(BF16)
    y = _dot(c, wo_ref[:C_CONV, :]) + _dot(a_ref[0], wo_ref[C_CONV:, :]) + bo_ref[...]
    o_ref[0] = _norm(_norm(x_ref[0], ALPHA) + g1_ref[0] * y) * lg_ref[...] + lb_ref[...]


def _mix_call(glu, attn, x, g1, w, tm):
    b, n, d = x.shape
    tok = lambda bi, i: (bi, i, 0)
    row = lambda bi, i: (bi, 0, 0)
    return pl.pallas_call(
        _mix_kernel,
        grid=(b, n // tm),
        in_specs=[*_halo_specs(tm, n, C_CONV, HALO),
                  pl.BlockSpec((1, tm, HEAD_PAIRS * LANES), tok),
                  pl.BlockSpec((1, tm, d), tok),
                  pl.BlockSpec((1, 1, d), row),
                  _const_spec(w["dww"].shape), _const_spec(w["dwb"].shape),
                  _const_spec(w["cg"].shape), _const_spec(w["cb"].shape),
                  _const_spec(w["wo"].shape), _const_spec(w["bo"].shape),
                  _const_spec(w["l1g"].shape), _const_spec(w["l1b"].shape)],
        out_specs=pl.BlockSpec((1, tm, d), tok),
        out_shape=jax.ShapeDtypeStruct((b, n, d), F32),
        scratch_shapes=[pltpu.VMEM((C_CONV // LANES, tm + 2 * HALO, LANES), F32),
                        pltpu.VMEM((tm, C_CONV), F32)],
        compiler_params=_params("parallel", "parallel"),
        name="mix",
    )(glu, glu, glu, attn, x, g1, w["dww"], w["dwb"], w["cg"], w["cb"], w["wo"], w["bo"],
      w["l1g"], w["l1b"])


def _ffn_kernel(xp_ref, x_ref, xn_ref, sc_ref, sh_ref, g2_ref, wup_ref, fw_ref, fb_ref, wdn_ref,
                bdn_ref, lg_ref, lb_ref, o_ref, h_ref, u_ref, act_ref):
    tm = x_ref.shape[1]
    has_prev, has_next = _halo_masks()
    scale = 1.0 + sc_ref[0]
    shift = sh_ref[0]
    h_ref[...] = jnp.concatenate([(xp_ref[0] * scale + shift) * has_prev,
                                  x_ref[0] * scale + shift,
                                  (xn_ref[0] * scale + shift) * has_next], axis=0).astype(BF16)

    groups = FF_CHUNK // LANES

    def conv(slot, col0, u):
        outs = []
        for j in range(groups):
            u_ref[slot + j] = u[:, j * LANES:(j + 1) * LANES]
            cols = slice(col0 + j * LANES, col0 + (j + 1) * LANES)
            acc = fb_ref[:, cols]
            for kk in range(FFN_K):
                acc = acc + u_ref[slot + j, pl.ds(FFN_HALO - 1 + kk, tm), :] * fw_ref[kk:kk + 1, cols]
            outs.append(acc)
        return outs

    for c in range(D_FF // FF_CHUNK):
        g = conv(4 * (c % 2), c * FF_CHUNK, _dot(h_ref[...], wup_ref[:, c * FF_CHUNK:(c + 1) * FF_CHUNK]))
        v = conv(4 * (c % 2) + 2, D_FF + c * FF_CHUNK,
                 _dot(h_ref[...], wup_ref[:, D_FF + c * FF_CHUNK:D_FF + (c + 1) * FF_CHUNK]))
        for j in range(groups):
            cols = slice(c * FF_CHUNK + j * LANES, c * FF_CHUNK + (j + 1) * LANES)
            act_ref[:, cols] = (_silu(g[j]) * v[j]).astype(BF16)

    for r0 in range(0, tm, tm // 2):
        rows = slice(r0, r0 + tm // 2)
        f = _dot(act_ref[rows, :], wdn_ref[...]) + bdn_ref[...]
        o_ref[0, rows] = _norm(ALPHA * x_ref[0, rows] + g2_ref[0] * f) * lg_ref[...] + lb_ref[...]


def _ffn_call(xm, sc, sh, g2, w, tm):
    b, n, d = xm.shape
    tok = lambda bi, i: (bi, i, 0)
    row = lambda bi, i: (bi, 0, 0)
    return pl.pallas_call(
        _ffn_kernel,
        grid=(b, n // tm),
        in_specs=[*_halo_specs(tm, n, d, FFN_HALO),
                  pl.BlockSpec((1, 1, d), row), pl.BlockSpec((1, 1, d), row), pl.BlockSpec((1, 1, d), row),
                  _const_spec(w["wup"].shape), _const_spec(w["fw"].shape), _const_spec(w["fb"].shape),
                  _const_spec(w["wdn"].shape), _const_spec(w["bdn"].shape),
                  _const_spec(w["l2g"].shape), _const_spec(w["l2b"].shape)],
        out_specs=pl.BlockSpec((1, tm, d), tok),
        out_shape=jax.ShapeDtypeStruct((b, n, d), F32),
        scratch_shapes=[pltpu.VMEM((tm + 2 * FFN_HALO, d), BF16),
                        pltpu.VMEM((4 * FF_CHUNK // LANES, tm + 2 * FFN_HALO, LANES), F32),
                        pltpu.VMEM((tm, D_FF), BF16)],
        compiler_params=_params("parallel", "parallel"),
        name="ffn",
    )(xm, xm, xm, sc, sh, g2, w["wup"], w["fw"], w["fb"], w["wdn"], w["bdn"], w["l2g"], w["l2b"])


def _rope_tables(n):
    t = jnp.arange(n)
    n_freq = QK_ROPE // 4
    inv = ROPE_THETA ** (-jnp.arange(n_freq, dtype=F32) / n_freq)
    ang = jnp.concatenate([(t // GRID_W)[:, None] * inv, (t % GRID_W)[:, None] * inv], axis=-1)
    cos, sin = jnp.cos(ang), jnp.sin(ang)
    pad = jnp.zeros((n, LANES - QK_NOPE - QK_ROPE), F32)
    cos_t = jnp.concatenate([jnp.ones((n, QK_NOPE), F32), cos, cos, pad], axis=-1)
    sin_t = jnp.concatenate([jnp.zeros((n, QK_NOPE), F32), sin, sin, pad], axis=-1)
    return cos_t, sin_t


def _partner(w_rope):
    half = w_rope.shape[-1] // 2
    return jnp.concatenate([-w_rope[..., half:], w_rope[..., :half]], axis=-1)


def _layout_weights(w_in, w_uq, w_ukv):
    d = w_in.shape[0]
    tail = LANES - QK_NOPE - QK_ROPE
    w_in = w_in.astype(BF16)
    z = lambda r, c: jnp.zeros((r, c), BF16)
    kr = w_in[:, _Z_KR:_Z_KR + QK_ROPE]
    win = jnp.concatenate([w_in[:, :_Z_KR], z(d, QK_NOPE), kr, _partner(kr)], axis=-1)
    win_ctx = jnp.concatenate([w_in[:, _Z_CKV:_Z_KR], z(d, QK_NOPE), kr, z(d, tail)], axis=-1)

    wq = w_uq.reshape(Q_LORA, N_HEADS, QK_NOPE + QK_ROPE)
    wq = jnp.concatenate([wq, _partner(wq[..., QK_NOPE:])], axis=-1).reshape(Q_LORA, N_HEADS * LANES)
    wkv = w_ukv.reshape(KV_LORA, N_HEADS, QK_NOPE + V_DIM)
    wk = wkv[..., :QK_NOPE].reshape(KV_LORA, N_HEADS * QK_NOPE)
    wv = wkv[..., QK_NOPE:].reshape(KV_LORA, N_HEADS * V_DIM)
    cast = lambda a: a.astype(BF16)
    return dict(win=cast(win), win_ctx=cast(win_ctx), wq=cast(wq), wk=cast(wk), wv=cast(wv))


def kernel(x, c, ctx, c_ctx, w_ada, b_ada, w_in, conv_dw_w, conv_dw_b, conv_ln_g, conv_ln_b, q_norm_g,
           w_uq, kv_norm_g, w_ukv, w_o, b_o, ln1_g, ln1_b, w_up, ffn_dw_w, ffn_dw_b, w_down, b_down,
           ln2_g, ln2_b):
    b, n, d = x.shape
    tm = 512
    tq = 2 * ATTN_ROWS
    assert DEPTH == 1 and w_ada.shape[0] == 1
    assert n % tm == 0 and n % tq == 0 and tm % HALO == 0

    c_rows = jnp.concatenate([c, c_ctx[None, :], jnp.zeros((16 - b - 1, d), F32)], axis=0)
    mod = _mod_call(c_rows, w_ada[0], b_ada).reshape(16, 6, 1, d)
    sh1, sc1, g1, sh2, sc2, g2 = (mod[:b, j] for j in range(6))
    sh1c, sc1c = mod[b:b + 1, 0], mod[b:b + 1, 1]

    row2 = lambda a: a.reshape(1, -1)
    w = _layout_weights(w_in[0], w_uq[0], w_ukv[0])
    w.update(gq=row2(q_norm_g[0]), gkv=row2(kv_norm_g[0]))
    cos_t, sin_t = _rope_tables(n)

    glu, q, k_lat, v_lat = _proj_call(x, sc1, sh1, cos_t, sin_t, w, 2 * PROJ_ROWS)
    k_ctx, v_ctx = _proj_ctx_call(ctx, sc1c, sh1c, w)
    attn = _attn_call(q, k_lat, v_lat, k_ctx, v_ctx, tq)

    wm = dict(dww=conv_dw_w[0], dwb=row2(conv_dw_b[0]), cg=row2(conv_ln_g[0]), cb=row2(conv_ln_b[0]),
              wo=w_o[0].astype(BF16), bo=row2(b_o[0]), l1g=row2(ln1_g[0]), l1b=row2(ln1_b[0]))
    x_mid = _mix_call(glu, attn, x, g1, wm, 2 * tm)

    wf = dict(wup=w_up[0].astype(BF16), fw=ffn_dw_w[0], fb=row2(ffn_dw_b[0]), wdn=w_down[0].astype(BF16),
              bdn=row2(b_down[0]), l2g=row2(ln2_g[0]), l2b=row2(ln2_b[0]))
    return _ffn_call(x_mid, sc2, sh2, g2, wf, tm)
```

```python
import math

import jax
import jax.numpy as jnp
from jax import lax
from jax.experimental import pallas as pl
from jax.experimental.pallas import tpu as pltpu

F32 = jnp.float32
BF16 = jnp.bfloat16

D_MODEL = 1024
GRID_W = 64
C_CONV = 512
CONV_K = 31
N_HEADS = 8
QK_NOPE = 64
QK_ROPE = 32
V_DIM = 64
Q_LORA = 384
KV_LORA = 256
D_FF = 2816
FFN_K = 3
ROPE_THETA = 10000.0
LN_EPS = 1e-5
DEPTH = 1
ALPHA = (2 * DEPTH) ** 0.25
SOFTMAX_SCALE = (QK_NOPE + QK_ROPE) ** -0.5
LOG2E = math.log2(math.e)

LANES = 128
SUBLANES = 8
HEAD_PAIRS = N_HEADS // 2
V_TILE = 2 * LANES
HALO = 16
FFN_HALO = SUBLANES
FF_CHUNK = 256
KEY_CHUNK = 256
ATTN_ROWS = 512
PROJ_ROWS = 512
VMEM_LIMIT = 52 * 1024 * 1024

_Z_CQ = 2 * C_CONV
_Z_CKV = _Z_CQ + Q_LORA
_Z_KR = _Z_CKV + KV_LORA
_Z_END = _Z_KR + LANES
PARTNER_ROLL = LANES - QK_ROPE


def _sigmoid(x):
    return 0.5 * jnp.tanh(0.5 * x) + 0.5


def _silu(x):
    h = 0.5 * x
    return h * jnp.tanh(h) + h


def _norm(x, gain=1.0):
    mu = jnp.mean(x, axis=-1, keepdims=True)
    xc = x - mu
    var = jnp.mean(xc * xc, axis=-1, keepdims=True)
    return xc * (lax.rsqrt(var + LN_EPS) * gain)


def _rms(x, g):
    return x * lax.rsqrt(jnp.mean(x * x, axis=-1, keepdims=True) + LN_EPS) * g


def _dot(a, b):
    return jnp.dot(a, b, preferred_element_type=F32)


def _params(*sem):
    return pltpu.CompilerParams(dimension_semantics=sem, vmem_limit_bytes=VMEM_LIMIT)


def _const_spec(shape):
    zeros = (0,) * len(shape)
    return pl.BlockSpec(shape, lambda *_: zeros)


def _mod_kernel(c_ref, w_ref, b_ref, o_ref):
    c = c_ref[...]
    a = _silu(c)
    o_ref[...] = jnp.dot(a, w_ref[...], precision=lax.Precision.HIGHEST,
                         preferred_element_type=F32) + b_ref[...]


def _mod_call(c_rows, w_ada, b_ada):
    rows, d = c_rows.shape
    n = w_ada.shape[1]
    bn = 1536
    return pl.pallas_call(
        _mod_kernel,
        grid=(n // bn,),
        in_specs=[pl.BlockSpec((rows, d), lambda j: (0, 0)),
                  pl.BlockSpec((d, bn), lambda j: (0, j)),
                  pl.BlockSpec((1, bn), lambda j: (0, j))],
        out_specs=pl.BlockSpec((rows, bn), lambda j: (0, j)),
        out_shape=jax.ShapeDtypeStruct((rows, n), F32),
        compiler_params=_params("arbitrary"),
        name="mod",
    )(c_rows, w_ada, b_ada)


def _store_kv(ckv, kr128, gkv_ref, wk_ref, wv_ref, k_ref, v_ref, rows):
    ckvn = _rms(ckv, gkv_ref[...]).astype(BF16)
    kk = _dot(ckvn, wk_ref[...])
    lane = lax.broadcasted_iota(jnp.int32, kr128.shape, 1)
    for h in range(N_HEADS):
        pair = kk[:, (h // 2) * LANES:(h // 2 + 1) * LANES]
        if h % 2:
            pair = pltpu.roll(pair, QK_NOPE, 1)
        k_ref[0, h, rows] = jnp.where(lane < QK_NOPE, pair, kr128).astype(BF16)
    vv = _dot(ckvn, wv_ref[...])
    one_col = (lax.broadcasted_iota(jnp.int32, (vv.shape[0], LANES), 1) == 0).astype(BF16)
    for p in range(HEAD_PAIRS):
        v_ref[0, p, rows, :LANES] = vv[:, p * LANES:(p + 1) * LANES].astype(BF16)
        v_ref[0, p, rows, LANES:] = one_col


def _rotary(t, cos, sin):
    return t * cos + pltpu.roll(t, PARTNER_ROLL, 1) * sin


def _proj_kernel(x_ref, sc_ref, sh_ref, cos_ref, sin_ref, win_ref, gq_ref, wq_ref,
                 gkv_ref, wk_ref, wv_ref, glu_ref, q_ref, k_ref, v_ref):
    for r0 in range(0, x_ref.shape[1], PROJ_ROWS):
        rows = slice(r0, r0 + PROJ_ROWS)
        h = _norm(x_ref[0, rows]) * (1.0 + sc_ref[0]) + sh_ref[0]
        z = _dot(h.astype(BF16), win_ref[...])
        glu_ref[0, rows] = z[:, :C_CONV] * _sigmoid(z[:, C_CONV:_Z_CQ])
        cos = cos_ref[rows]
        sin = sin_ref[rows]
        cqn = _rms(z[:, _Z_CQ:_Z_CKV], gq_ref[...]).astype(BF16)
        qa = _dot(cqn, wq_ref[...])
        for hd in range(N_HEADS):
            q = _rotary(qa[:, hd * LANES:(hd + 1) * LANES], cos, sin) * (SOFTMAX_SCALE * LOG2E)
            q_ref[0, hd, rows] = q.astype(BF16)
        kr128 = _rotary(z[:, _Z_KR:_Z_END], cos, sin)
        _store_kv(z[:, _Z_CKV:_Z_KR], kr128, gkv_ref, wk_ref, wv_ref, k_ref, v_ref, rows)


def _proj_ctx_kernel(x_ref, sc_ref, sh_ref, win_ref, gkv_ref, wk_ref, wv_ref, k_ref, v_ref):
    h = _norm(x_ref[0]) * (1.0 + sc_ref[0]) + sh_ref[0]
    z = _dot(h.astype(BF16), win_ref[...])
    _store_kv(z[:, :KV_LORA], z[:, KV_LORA:], gkv_ref, wk_ref, wv_ref, k_ref, v_ref, slice(None))


def _proj_call(x, sc, sh, cos_t, sin_t, w, tm):
    b, n, d = x.shape
    nt = n // tm
    tok = lambda bi, i: (bi, i, 0)
    row = lambda bi, i: (bi, 0, 0)
    tab = lambda bi, i: (i, 0)
    hd4 = lambda bi, i: (bi, 0, i, 0)
    return pl.pallas_call(
        _proj_kernel,
        grid=(b, nt),
        in_specs=[pl.BlockSpec((1, tm, d), tok),
                  pl.BlockSpec((1, 1, d), row), pl.BlockSpec((1, 1, d), row),
                  pl.BlockSpec((tm, LANES), tab), pl.BlockSpec((tm, LANES), tab),
                  _const_spec(w["win"].shape), _const_spec(w["gq"].shape), _const_spec(w["wq"].shape),
                  _const_spec(w["gkv"].shape), _const_spec(w["wk"].shape), _const_spec(w["wv"].shape)],
        out_specs=[pl.BlockSpec((1, tm, C_CONV), tok),
                   pl.BlockSpec((1, N_HEADS, tm, LANES), hd4),
                   pl.BlockSpec((1, N_HEADS, tm, LANES), hd4),
                   pl.BlockSpec((1, HEAD_PAIRS, tm, V_TILE), hd4)],
        out_shape=[jax.ShapeDtypeStruct((b, n, C_CONV), F32),
                   jax.ShapeDtypeStruct((b, N_HEADS, n, LANES), BF16),
                   jax.ShapeDtypeStruct((b, N_HEADS, n, LANES), BF16),
                   jax.ShapeDtypeStruct((b, HEAD_PAIRS, n, V_TILE), BF16)],
        compiler_params=_params("parallel", "parallel"),
        name="proj",
    )(x, sc, sh, cos_t, sin_t, w["win"], w["gq"], w["wq"], w["gkv"], w["wk"], w["wv"])


def _proj_ctx_call(ctx, sc, sh, w):
    b, n_ctx, d = ctx.shape
    row = lambda bi: (0, 0, 0)
    kv = lambda bi: (bi, 0, 0, 0)
    return pl.pallas_call(
        _proj_ctx_kernel,
        grid=(b,),
        in_specs=[pl.BlockSpec((1, n_ctx, d), lambda bi: (bi, 0, 0)),
                  pl.BlockSpec((1, 1, d), row), pl.BlockSpec((1, 1, d), row),
                  _const_spec(w["win_ctx"].shape), _const_spec(w["gkv"].shape),
                  _const_spec(w["wk"].shape), _const_spec(w["wv"].shape)],
        out_specs=[pl.BlockSpec((1, N_HEADS, n_ctx, LANES), kv),
                   pl.BlockSpec((1, HEAD_PAIRS, n_ctx, V_TILE), kv)],
        out_shape=[jax.ShapeDtypeStruct((b, N_HEADS, n_ctx, LANES), BF16),
                   jax.ShapeDtypeStruct((b, HEAD_PAIRS, n_ctx, V_TILE), BF16)],
        compiler_params=_params("parallel"),
        name="proj_ctx",
    )(ctx, sc, sh, w["win_ctx"], w["gkv"], w["wk"], w["wv"])


def _attn_kernel(q_ref, k_ref, v_ref, kc_ref, vc_ref, o_ref):
    chunks = [(k_ref, v_ref, c0) for c0 in range(0, k_ref.shape[2], KEY_CHUNK)]
    chunks += [(kc_ref, vc_ref, c0) for c0 in range(0, kc_ref.shape[2], KEY_CHUNK)]
    n_chunks = len(chunks)
    n_sub = q_ref.shape[2] // ATTN_ROWS

    def scores(unit, c):
        sub, j = unit
        keys, _, c0 = chunks[c]
        return lax.dot_general(q_ref[0, j, sub * ATTN_ROWS:(sub + 1) * ATTN_ROWS],
                               keys[0, j, c0:c0 + KEY_CHUNK],
                               (((1,), (1,)), ((), ())), preferred_element_type=F32)

    def row_max(s):
        m = s[0]
        for sc in s[1:]:
            m = jnp.maximum(m, sc)
        return jnp.max(m, axis=-1, keepdims=True)

    def weighted_values(s, m, c, acc):
        _, values, c0 = chunks[c]
        pv = _dot(jnp.exp2(s[c] - m).astype(BF16), values[0, 0, c0:c0 + KEY_CHUNK])
        return pv if acc is None else acc + pv

    units = [(sub, j) for sub in range(n_sub) for j in range(2)]
    s_cur = [scores(units[0], c) for c in range(n_chunks)]
    outs = []
    for u in range(len(units)):
        m_cur, acc, s_nxt = row_max(s_cur), None, []
        for c in range(n_chunks):
            if u + 1 < len(units):
                s_nxt.append(scores(units[u + 1], c))
            acc = weighted_values(s_cur, m_cur, c, acc)
        outs.append(acc[:, :LANES] * (1.0 / acc[:, LANES:LANES + 1]))
        s_cur = s_nxt
    lane = lax.broadcasted_iota(jnp.int32, outs[0].shape, 1)
    for sub in range(n_sub):
        o_ref[0, sub * ATTN_ROWS:(sub + 1) * ATTN_ROWS] = jnp.where(
            lane < V_DIM, outs[2 * sub], outs[2 * sub + 1]).astype(BF16)


def _attn_call(q, k, v, kc, vc, tq):
    b, _, n, _ = q.shape
    nk, nc = k.shape[2], kc.shape[2]
    assert nk % KEY_CHUNK == 0 and nc % KEY_CHUNK == 0
    return pl.pallas_call(
        _attn_kernel,
        grid=(b, HEAD_PAIRS, n // tq),
        in_specs=[pl.BlockSpec((1, 2, tq, LANES), lambda bi, p, i: (bi, p, i, 0)),
                  pl.BlockSpec((1, 2, nk, LANES), lambda bi, p, i: (bi, p, 0, 0)),
                  pl.BlockSpec((1, 1, nk, V_TILE), lambda bi, p, i: (bi, p, 0, 0)),
                  pl.BlockSpec((1, 2, nc, LANES), lambda bi, p, i: (bi, p, 0, 0)),
                  pl.BlockSpec((1, 1, nc, V_TILE), lambda bi, p, i: (bi, p, 0, 0))],
        out_specs=pl.BlockSpec((1, tq, LANES), lambda bi, p, i: (bi, i, p)),
        out_shape=jax.ShapeDtypeStruct((b, n, HEAD_PAIRS * LANES), BF16),
        compiler_params=_params("parallel", "parallel", "arbitrary"),
        name="attn",
    )(q, k, v, kc, vc)


def _halo_specs(tm, n, width, halo):
    per = tm // halo
    last = n // halo - 1
    return (pl.BlockSpec((1, halo, width), lambda bi, i: (bi, jnp.maximum(i * per - 1, 0), 0)),
            pl.BlockSpec((1, tm, width), lambda bi, i: (bi, i, 0)),
            pl.BlockSpec((1, halo, width), lambda bi, i: (bi, jnp.minimum((i + 1) * per, last), 0)))


def _halo_masks():
    i = pl.program_id(1)
    has_prev = (i > 0).astype(F32)
    has_next = (i < pl.num_programs(1) - 1).astype(F32)
    return has_prev, has_next


CONV_ROWS = 64


def _mix_kernel(gp_ref, g_ref, gn_ref, a_ref, x_ref, g1_ref, dww_ref, dwb_ref, cg_ref, cb_ref,
                wo_ref, bo_ref, lg_ref, lb_ref, o_ref, win_ref, conv_ref):
    tm = g_ref.shape[1]
    has_prev, has_next = _halo_masks()
    for j in range(C_CONV // LANES):
        cols = slice(j * LANES, (j + 1) * LANES)
        win_ref[j, 0:HALO] = gp_ref[0, :, cols] * has_prev
        win_ref[j, HALO:HALO + tm] = g_ref[0, :, cols]
        win_ref[j, HALO + tm:] = gn_ref[0, :, cols] * has_next
    half = CONV_K // 2

    def rows(c, carry):
        base = pl.multiple_of(c * CONV_ROWS, CONV_ROWS)
        for j in range(C_CONV // LANES):
            cols = slice(j * LANES, (j + 1) * LANES)
            acc = jnp.zeros((CONV_ROWS, LANES), F32) + dwb_ref[:, cols]
            for kk in range(CONV_K):
                acc = acc + (win_ref[j, pl.ds(base + (HALO - half + kk), CONV_ROWS), :]
                             * dww_ref[kk:kk + 1, cols])
            conv_ref[pl.ds(base, CONV_ROWS), cols] = acc
        return carry

    lax.fori_loop(0, tm // CONV_ROWS, rows, 0)
    c = _silu(_norm(conv_ref[...]) * cg_ref[...] + cb_ref[...]).astype(BF16)
    y = _dot(c, wo_ref[:C_CONV, :]) + _dot(a_ref[0], wo_ref[C_CONV:, :]) + bo_ref[...]
    o_ref[0] = _norm(_norm(x_ref[0], ALPHA) + g1_ref[0] * y) * lg_ref[...] + lb_ref[...]


def _mix_call(glu, attn, x, g1, w, tm):
    b, n, d = x.shape
    tok = lambda bi, i: (bi, i, 0)
    row = lambda bi, i: (bi, 0, 0)
    return pl.pallas_call(
        _mix_kernel,
        grid=(b, n // tm),
        in_specs=[*_halo_specs(tm, n, C_CONV, HALO),
                  pl.BlockSpec((1, tm, HEAD_PAIRS * LANES), tok),
                  pl.BlockSpec((1, tm, d), tok),
                  pl.BlockSpec((1, 1, d), row),
                  _const_spec(w["dww"].shape), _const_spec(w["dwb"].shape),
                  _const_spec(w["cg"].shape), _const_spec(w["cb"].shape),
                  _const_spec(w["wo"].shape), _const_spec(w["bo"].shape),
                  _const_spec(w["l1g"].shape), _const_spec(w["l1b"].shape)],
        out_specs=pl.BlockSpec((1, tm, d), tok),
        out_shape=jax.ShapeDtypeStruct((b, n, d), F32),
        scratch_shapes=[pltpu.VMEM((C_CONV // LANES, tm + 2 * HALO, LANES), F32),
                        pltpu.VMEM((tm, C_CONV), F32)],
        compiler_params=_params("parallel", "parallel"),
        name="mix",
    )(glu, glu, glu, attn, x, g1, w["dww"], w["dwb"], w["cg"], w["cb"], w["wo"], w["bo"],
      w["l1g"], w["l1b"])


def _ffn_kernel(xp_ref, x_ref, xn_ref, sc_ref, sh_ref, g2_ref, wup_ref, fw_ref, fb_ref, wdn_ref,
                bdn_ref, lg_ref, lb_ref, o_ref, h_ref, u_ref, act_ref):
    tm = x_ref.shape[1]
    has_prev, has_next = _halo_masks()
    scale = 1.0 + sc_ref[0]
    shift = sh_ref[0]
    h_ref[...] = jnp.concatenate([(xp_ref[0] * scale + shift) * has_prev,
                                  x_ref[0] * scale + shift,
                                  (xn_ref[0] * scale + shift) * has_next], axis=0).astype(BF16)

    groups = FF_CHUNK // LANES

    def conv(slot, col0, u):
        outs = []
        for j in range(groups):
            u_ref[slot + j] = u[:, j * LANES:(j + 1) * LANES]
            cols = slice(col0 + j * LANES, col0 + (j + 1) * LANES)
            acc = fb_ref[:, cols]
            for kk in range(FFN_K):
                acc = acc + u_ref[slot + j, pl.ds(FFN_HALO - 1 + kk, tm), :] * fw_ref[kk:kk + 1, cols]
            outs.append(acc)
        return outs

    for c in range(D_FF // FF_CHUNK):
        g = conv(4 * (c % 2), c * FF_CHUNK, _dot(h_ref[...], wup_ref[:, c * FF_CHUNK:(c + 1) * FF_CHUNK]))
        v = conv(4 * (c % 2) + 2, D_FF + c * FF_CHUNK,
                 _dot(h_ref[...], wup_ref[:, D_FF + c * FF_CHUNK:D_FF + (c + 1) * FF_CHUNK]))
        for j in range(groups):
            cols = slice(c * FF_CHUNK + j * LANES, c * FF_CHUNK + (j + 1) * LANES)
            act_ref[:, cols] = (_silu(g[j]) * v[j]).astype(BF16)

    for r0 in range(0, tm, tm // 2):
        rows = slice(r0, r0 + tm // 2)
        f = _dot(act_ref[rows, :], wdn_ref[...]) + bdn_ref[...]
        o_ref[0, rows] = _norm(ALPHA * x_ref[0, rows] + g2_ref[0] * f) * lg_ref[...] + lb_ref[...]


def _ffn_call(xm, sc, sh, g2, w, tm):
    b, n, d = xm.shape
    tok = lambda bi, i: (bi, i, 0)
    row = lambda bi, i: (bi, 0, 0)
    return pl.pallas_call(
        _ffn_kernel,
        grid=(b, n // tm),
        in_specs=[*_halo_specs(tm, n, d, FFN_HALO),
                  pl.BlockSpec((1, 1, d), row), pl.BlockSpec((1, 1, d), row), pl.BlockSpec((1, 1, d), row),
                  _const_spec(w["wup"].shape), _const_spec(w["fw"].shape), _const_spec(w["fb"].shape),
                  _const_spec(w["wdn"].shape), _const_spec(w["bdn"].shape),
                  _const_spec(w["l2g"].shape), _const_spec(w["l2b"].shape)],
        out_specs=pl.BlockSpec((1, tm, d), tok),
        out_shape=jax.ShapeDtypeStruct((b, n, d), F32),
        scratch_shapes=[pltpu.VMEM((tm + 2 * FFN_HALO, d), BF16),
                        pltpu.VMEM((4 * FF_CHUNK // LANES, tm + 2 * FFN_HALO, LANES), F32),
                        pltpu.VMEM((tm, D_FF), BF16)],
        compiler_params=_params("parallel", "parallel"),
        name="ffn",
    )(xm, xm, xm, sc, sh, g2, w["wup"], w["fw"], w["fb"], w["wdn"], w["bdn"], w["l2g"], w["l2b"])


def _rope_tables(n):
    t = jnp.arange(n)
    n_freq = QK_ROPE // 4
    inv = ROPE_THETA ** (-jnp.arange(n_freq, dtype=F32) / n_freq)
    ang = jnp.concatenate([(t // GRID_W)[:, None] * inv, (t % GRID_W)[:, None] * inv], axis=-1)
    cos, sin = jnp.cos(ang), jnp.sin(ang)
    pad = jnp.zeros((n, LANES - QK_NOPE - QK_ROPE), F32)
    cos_t = jnp.concatenate([jnp.ones((n, QK_NOPE), F32), cos, cos, pad], axis=-1)
    sin_t = jnp.concatenate([jnp.zeros((n, QK_NOPE), F32), sin, sin, pad], axis=-1)
    return cos_t, sin_t


def _partner(w_rope):
    half = w_rope.shape[-1] // 2
    return jnp.concatenate([-w_rope[..., half:], w_rope[..., :half]], axis=-1)


def _layout_weights(w_in, w_uq, w_ukv):
    d = w_in.shape[0]
    tail = LANES - QK_NOPE - QK_ROPE
    w_in = w_in.astype(BF16)
    z = lambda r, c: jnp.zeros((r, c), BF16)
    kr = w_in[:, _Z_KR:_Z_KR + QK_ROPE]
    win = jnp.concatenate([w_in[:, :_Z_KR], z(d, QK_NOPE), kr, _partner(kr)], axis=-1)
    win_ctx = jnp.concatenate([w_in[:, _Z_CKV:_Z_KR], z(d, QK_NOPE), kr, z(d, tail)], axis=-1)

    wq = w_uq.reshape(Q_LORA, N_HEADS, QK_NOPE + QK_ROPE)
    wq = jnp.concatenate([wq, _partner(wq[..., QK_NOPE:])], axis=-1).reshape(Q_LORA, N_HEADS * LANES)
    wkv = w_ukv.reshape(KV_LORA, N_HEADS, QK_NOPE + V_DIM)
    wk = wkv[..., :QK_NOPE].reshape(KV_LORA, N_HEADS * QK_NOPE)
    wv = wkv[..., QK_NOPE:].reshape(KV_LORA, N_HEADS * V_DIM)
    cast = lambda a: a.astype(BF16)
    return dict(win=cast(win), win_ctx=cast(win_ctx), wq=cast(wq), wk=cast(wk), wv=cast(wv))


def kernel(x, c, ctx, c_ctx, w_ada, b_ada, w_in, conv_dw_w, conv_dw_b, conv_ln_g, conv_ln_b, q_norm_g,
           w_uq, kv_norm_g, w_ukv, w_o, b_o, ln1_g, ln1_b, w_up, ffn_dw_w, ffn_dw_b, w_down, b_down,
           ln2_g, ln2_b):
    b, n, d = x.shape
    tm = 512
    tq = 4 * ATTN_ROWS
    assert DEPTH == 1 and w_ada.shape[0] == 1
    assert n % tm == 0 and n % tq == 0 and tm % HALO == 0

    c_rows = jnp.concatenate([c, c_ctx[None, :], jnp.zeros((16 - b - 1, d), F32)], axis=0)
    mod = _mod_call(c_rows, w_ada[0], b_ada).reshape(16, 6, 1, d)
    sh1, sc1, g1, sh2, sc2, g2 = (mod[:b, j] for j in range(6))
    sh1c, sc1c = mod[b:b + 1, 0], mod[b:b + 1, 1]

    row2 = lambda a: a.reshape(1, -1)
    w = _layout_weights(w_in[0], w_uq[0], w_ukv[0])
    w.update(gq=row2(q_norm_g[0]), gkv=row2(kv_norm_g[0]))
    cos_t, sin_t = _rope_tables(n)

    glu, q, k_lat, v_lat = _proj_call(x, sc1, sh1, cos_t, sin_t, w, 2 * PROJ_ROWS)
    k_ctx, v_ctx = _proj_ctx_call(ctx, sc1c, sh1c, w)
    attn = _attn_call(q, k_lat, v_lat, k_ctx, v_ctx, tq)

    wm = dict(dww=conv_dw_w[0], dwb=row2(conv_dw_b[0]), cg=row2(conv_ln_g[0]), cb=row2(conv_ln_b[0]),
              wo=w_o[0].astype(BF16), bo=row2(b_o[0]), l1g=row2(ln1_g[0]), l1b=row2(ln1_b[0]))
    x_mid = _mix_call(glu, attn, x, g1, wm, 2 * tm)

    wf = dict(wup=w_up[0].astype(BF16), fw=ffn_dw_w[0], fb=row2(ffn_dw_b[0]), wdn=w_down[0].astype(BF16),
              bdn=row2(b_down[0]), l2g=row2(ln2_g[0]), l2b=row2(ln2_b[0]))
    return _ffn_call(x_mid, sc2, sh2, g2, wf, tm)
```

```python
import math

import jax
import jax.numpy as jnp
from jax import lax
from jax.experimental import pallas as pl
from jax.experimental.pallas import tpu as pltpu

F32 = jnp.float32
BF16 = jnp.bfloat16

D_MODEL = 1024
GRID_W = 64
C_CONV = 512
CONV_K = 31
N_HEADS = 8
QK_NOPE = 64
QK_ROPE = 32
V_DIM = 64
Q_LORA = 384
KV_LORA = 256
D_FF = 2816
FFN_K = 3
ROPE_THETA = 10000.0
LN_EPS = 1e-5
DEPTH = 1
ALPHA = (2 * DEPTH) ** 0.25
SOFTMAX_SCALE = (QK_NOPE + QK_ROPE) ** -0.5
LOG2E = math.log2(math.e)

LANES = 128
SUBLANES = 8
HEAD_PAIRS = N_HEADS // 2
V_TILE = 2 * LANES
HALO = 16
FFN_HALO = SUBLANES
FF_CHUNK = 256
KEY_CHUNK = 256
ATTN_ROWS = 512
PROJ_ROWS = 512
VMEM_LIMIT = 52 * 1024 * 1024

_Z_CQ = 2 * C_CONV
_Z_CKV = _Z_CQ + Q_LORA
_Z_KR = _Z_CKV + KV_LORA
_Z_END = _Z_KR + LANES
PARTNER_ROLL = LANES - QK_ROPE


def _sigmoid(x):
    return 0.5 * jnp.tanh(0.5 * x) + 0.5


def _silu(x):
    h = 0.5 * x
    return h * jnp.tanh(h) + h


def _norm(x, gain=1.0):
    mu = jnp.mean(x, axis=-1, keepdims=True)
    xc = x - mu
    var = jnp.mean(xc * xc, axis=-1, keepdims=True)
    return xc * (lax.rsqrt(var + LN_EPS) * gain)


def _rms(x, g):
    return x * lax.rsqrt(jnp.mean(x * x, axis=-1, keepdims=True) + LN_EPS) * g


def _dot(a, b):
    return jnp.dot(a, b, preferred_element_type=F32)


def _params(*sem):
    return pltpu.CompilerParams(dimension_semantics=sem, vmem_limit_bytes=VMEM_LIMIT)


def _const_spec(shape):
    zeros = (0,) * len(shape)
    return pl.BlockSpec(shape, lambda *_: zeros)


def _mod_kernel(c_ref, w_ref, b_ref, o_ref):
    c = c_ref[...]
    a = _silu(c)
    o_ref[...] = jnp.dot(a, w_ref[...], precision=lax.Precision.HIGHEST,
                         preferred_element_type=F32) + b_ref[...]


def _mod_call(c_rows, w_ada, b_ada):
    rows, d = c_rows.shape
    n = w_ada.shape[1]
    bn = 1536
    return pl.pallas_call(
        _mod_kernel,
        grid=(n // bn,),
        in_specs=[pl.BlockSpec((rows, d), lambda j: (0, 0)),
                  pl.BlockSpec((d, bn), lambda j: (0, j)),
                  pl.BlockSpec((1, bn), lambda j: (0, j))],
        out_specs=pl.BlockSpec((rows, bn), lambda j: (0, j)),
        out_shape=jax.ShapeDtypeStruct((rows, n), F32),
        compiler_params=_params("arbitrary"),
        name="mod",
    )(c_rows, w_ada, b_ada)


def _store_kv(ckv, kr128, gkv_ref, wk_ref, wv_ref, k_ref, v_ref, rows):
    ckvn = _rms(ckv, gkv_ref[...]).astype(BF16)
    kk = _dot(ckvn, wk_ref[...])
    lane = lax.broadcasted_iota(jnp.int32, kr128.shape, 1)
    for h in range(N_HEADS):
        pair = kk[:, (h // 2) * LANES:(h // 2 + 1) * LANES]
        if h % 2:
            pair = pltpu.roll(pair, QK_NOPE, 1)
        k_ref[0, h, rows] = jnp.where(lane < QK_NOPE, pair, kr128).astype(BF16)
    vv = _dot(ckvn, wv_ref[...])
    one_col = (lax.broadcasted_iota(jnp.int32, (vv.shape[0], LANES), 1) == 0).astype(BF16)
    for p in range(HEAD_PAIRS):
        v_ref[0, p, rows, :LANES] = vv[:, p * LANES:(p + 1) * LANES].astype(BF16)
        v_ref[0, p, rows, LANES:] = one_col


def _rotary(t, cos, sin):
    return t * cos + pltpu.roll(t, PARTNER_ROLL, 1) * sin


def _proj_kernel(x_ref, sc_ref, sh_ref, cos_ref, sin_ref, win_ref, gq_ref, wq_ref,
                 gkv_ref, wk_ref, wv_ref, glu_ref, q_ref, k_ref, v_ref):
    for r0 in range(0, x_ref.shape[1], PROJ_ROWS):
        rows = slice(r0, r0 + PROJ_ROWS)
        h = _norm(x_ref[0, rows]) * (1.0 + sc_ref[0]) + sh_ref[0]
        z = _dot(h.astype(BF16), win_ref[...])
        glu_ref[0, rows] = z[:, :C_CONV] * _sigmoid(z[:, C_CONV:_Z_CQ])
        cos = cos_ref[rows]
        sin = sin_ref[rows]
        cqn = _rms(z[:, _Z_CQ:_Z_CKV], gq_ref[...]).astype(BF16)
        qa = _dot(cqn, wq_ref[...])
        for hd in range(N_HEADS):
            q = _rotary(qa[:, hd * LANES:(hd + 1) * LANES], cos, sin) * (SOFTMAX_SCALE * LOG2E)
            q_ref[0, hd, rows] = q.astype(BF16)
        kr128 = _rotary(z[:, _Z_KR:_Z_END], cos, sin)
        _store_kv(z[:, _Z_CKV:_Z_KR], kr128, gkv_ref, wk_ref, wv_ref, k_ref, v_ref, rows)


def _proj_ctx_kernel(x_ref, sc_ref, sh_ref, win_ref, gkv_ref, wk_ref, wv_ref, k_ref, v_ref):
    h = _norm(x_ref[0]) * (1.0 + sc_ref[0]) + sh_ref[0]
    z = _dot(h.astype(BF16), win_ref[...])
    _store_kv(z[:, :KV_LORA], z[:, KV_LORA:], gkv_ref, wk_ref, wv_ref, k_ref, v_ref, slice(None))


def _proj_call(x, sc, sh, cos_t, sin_t, w, tm):
    b, n, d = x.shape
    nt = n // tm
    tok = lambda bi, i: (bi, i, 0)
    row = lambda bi, i: (bi, 0, 0)
    tab = lambda bi, i: (i, 0)
    hd4 = lambda bi, i: (bi, 0, i, 0)
    return pl.pallas_call(
        _proj_kernel,
        grid=(b, nt),
        in_specs=[pl.BlockSpec((1, tm, d), tok),
                  pl.BlockSpec((1, 1, d), row), pl.BlockSpec((1, 1, d), row),
                  pl.BlockSpec((tm, LANES), tab), pl.BlockSpec((tm, LANES), tab),
                  _const_spec(w["win"].shape), _const_spec(w["gq"].shape), _const_spec(w["wq"].shape),
                  _const_spec(w["gkv"].shape), _const_spec(w["wk"].shape), _const_spec(w["wv"].shape)],
        out_specs=[pl.BlockSpec((1, tm, C_CONV), tok),
                   pl.BlockSpec((1, N_HEADS, tm, LANES), hd4),
                   pl.BlockSpec((1, N_HEADS, tm, LANES), hd4),
                   pl.BlockSpec((1, HEAD_PAIRS, tm, V_TILE), hd4)],
        out_shape=[jax.ShapeDtypeStruct((b, n, C_CONV), F32),
                   jax.ShapeDtypeStruct((b, N_HEADS, n, LANES), BF16),
                   jax.ShapeDtypeStruct((b, N_HEADS, n, LANES), BF16),
                   jax.ShapeDtypeStruct((b, HEAD_PAIRS, n, V_TILE), BF16)],
        compiler_params=_params("parallel", "parallel"),
        name="proj",
    )(x, sc, sh, cos_t, sin_t, w["win"], w["gq"], w["wq"], w["gkv"], w["wk"], w["wv"])


def _proj_ctx_call(ctx, sc, sh, w):
    b, n_ctx, d = ctx.shape
    row = lambda bi: (0, 0, 0)
    kv = lambda bi: (bi, 0, 0, 0)
    return pl.pallas_call(
        _proj_ctx_kernel,
        grid=(b,),
        in_specs=[pl.BlockSpec((1, n_ctx, d), lambda bi: (bi, 0, 0)),
                  pl.BlockSpec((1, 1, d), row), pl.BlockSpec((1, 1, d), row),
                  _const_spec(w["win_ctx"].shape), _const_spec(w["gkv"].shape),
                  _const_spec(w["wk"].shape), _const_spec(w["wv"].shape)],
        out_specs=[pl.BlockSpec((1, N_HEADS, n_ctx, LANES), kv),
                   pl.BlockSpec((1, HEAD_PAIRS, n_ctx, V_TILE), kv)],
        out_shape=[jax.ShapeDtypeStruct((b, N_HEADS, n_ctx, LANES), BF16),
                   jax.ShapeDtypeStruct((b, HEAD_PAIRS, n_ctx, V_TILE), BF16)],
        compiler_params=_params("parallel"),
        name="proj_ctx",
    )(ctx, sc, sh, w["win_ctx"], w["gkv"], w["wk"], w["wv"])


def _attn_kernel(q_ref, k_ref, v_ref, kc_ref, vc_ref, o_ref):
    chunks = [(k_ref, v_ref, c0) for c0 in range(0, k_ref.shape[2], KEY_CHUNK)]
    chunks += [(kc_ref, vc_ref, c0) for c0 in range(0, kc_ref.shape[2], KEY_CHUNK)]
    n_chunks = len(chunks)
    n_sub = q_ref.shape[2] // ATTN_ROWS

    def scores(unit, c):
        sub, j = unit
        keys, _, c0 = chunks[c]
        return lax.dot_general(q_ref[0, j, sub * ATTN_ROWS:(sub + 1) * ATTN_ROWS],
                               keys[0, j, c0:c0 + KEY_CHUNK],
                               (((1,), (1,)), ((), ())), preferred_element_type=F32)

    def row_max(s):
        m = s[0]
        for sc in s[1:]:
            m = jnp.maximum(m, sc)
        return jnp.max(m, axis=-1, keepdims=True)

    def weighted_values(s, m, c, acc):
        _, values, c0 = chunks[c]
        pv = _dot(jnp.exp2(s[c] - m).astype(BF16), values[0, 0, c0:c0 + KEY_CHUNK])
        return pv if acc is None else acc + pv

    units = [(sub, j) for sub in range(n_sub) for j in range(2)]
    s_cur = [scores(units[0], c) for c in range(n_chunks)]
    outs = []
    for u in range(len(units)):
        m_cur, acc, s_nxt = row_max(s_cur), None, []
        for c in range(n_chunks):
            if u + 1 < len(units):
                s_nxt.append(scores(units[u + 1], c))
            acc = weighted_values(s_cur, m_cur, c, acc)
        outs.append(acc[:, :LANES] * (1.0 / acc[:, LANES:LANES + 1]))
        s_cur = s_nxt
    lane = lax.broadcasted_iota(jnp.int32, outs[0].shape, 1)
    for sub in range(n_sub):
        o_ref[0, sub * ATTN_ROWS:(sub + 1) * ATTN_ROWS] = jnp.where(
            lane < V_DIM, outs[2 * sub], outs[2 * sub + 1]).astype(BF16)


def _attn_call(q, k, v, kc, vc, tq):
    b, _, n, _ = q.shape
    nk, nc = k.shape[2], kc.shape[2]
    assert nk % KEY_CHUNK == 0 and nc % KEY_CHUNK == 0
    return pl.pallas_call(
        _attn_kernel,
        grid=(b, HEAD_PAIRS, n // tq),
        in_specs=[pl.BlockSpec((1, 2, tq, LANES), lambda bi, p, i: (bi, p, i, 0)),
                  pl.BlockSpec((1, 2, nk, LANES), lambda bi, p, i: (bi, p, 0, 0)),
                  pl.BlockSpec((1, 1, nk, V_TILE), lambda bi, p, i: (bi, p, 0, 0)),
                  pl.BlockSpec((1, 2, nc, LANES), lambda bi, p, i: (bi, p, 0, 0)),
                  pl.BlockSpec((1, 1, nc, V_TILE), lambda bi, p, i: (bi, p, 0, 0))],
        out_specs=pl.BlockSpec((1, tq, LANES), lambda bi, p, i: (bi, i, p)),
        out_shape=jax.ShapeDtypeStruct((b, n, HEAD_PAIRS * LANES), BF16),
        compiler_params=_params("parallel", "parallel", "arbitrary"),
        name="attn",
    )(q, k, v, kc, vc)


def _halo_specs(tm, n, width, halo):
    per = tm // halo
    last = n // halo - 1
    return (pl.BlockSpec((1, halo, width), lambda bi, i: (bi, jnp.maximum(i * per - 1, 0), 0)),
            pl.BlockSpec((1, tm, width), lambda bi, i: (bi, i, 0)),
            pl.BlockSpec((1, halo, width), lambda bi, i: (bi, jnp.minimum((i + 1) * per, last), 0)))


def _halo_masks():
    i = pl.program_id(1)
    has_prev = (i > 0).astype(F32)
    has_next = (i < pl.num_programs(1) - 1).astype(F32)
    return has_prev, has_next


CONV_ROWS = 64


def _mix_kernel(gp_ref, g_ref, gn_ref, a_ref, x_ref, g1_ref, dww_ref, dwb_ref, cg_ref, cb_ref,
                wo_ref, bo_ref, lg_ref, lb_ref, o_ref, win_ref, conv_ref):
    tm = g_ref.shape[1]
    has_prev, has_next = _halo_masks()
    for j in range(C_CONV // LANES):
        cols = slice(j * LANES, (j + 1) * LANES)
        win_ref[j, 0:HALO] = gp_ref[0, :, cols] * has_prev
        win_ref[j, HALO:HALO + tm] = g_ref[0, :, cols]
        win_ref[j, HALO + tm:] = gn_ref[0, :, cols] * has_next
    half = CONV_K // 2

    def rows(c, carry):
        base = pl.multiple_of(c * CONV_ROWS, CONV_ROWS)
        for j in range(C_CONV // LANES):
            cols = slice(j * LANES, (j + 1) * LANES)
            acc = jnp.zeros((CONV_ROWS, LANES), F32) + dwb_ref[:, cols]
            for kk in range(CONV_K):
                acc = acc + (win_ref[j, pl.ds(base + (HALO - half + kk), CONV_ROWS), :]
                             * dww_ref[kk:kk + 1, cols])
            conv_ref[pl.ds(base, CONV_ROWS), cols] = acc
        return carry

    lax.fori_loop(0, tm // CONV_ROWS, rows, 0)
    c = _silu(_norm(conv_ref[...]) * cg_ref[...] + cb_ref[...]).astype(BF16)
    y = _dot(c, wo_ref[:C_CONV, :]) + _dot(a_ref[0], wo_ref[C_CONV:, :]) + bo_ref[...]
    o_ref[0] = _norm(_norm(x_ref[0], ALPHA) + g1_ref[0] * y) * lg_ref[...] + lb_ref[...]


def _mix_call(glu, attn, x, g1, w, tm):
    b, n, d = x.shape
    tok = lambda bi, i: (bi, i, 0)
    row = lambda bi, i: (bi, 0, 0)
    return pl.pallas_call(
        _mix_kernel,
        grid=(b, n // tm),
        in_specs=[*_halo_specs(tm, n, C_CONV, HALO),
                  pl.BlockSpec((1, tm, HEAD_PAIRS * LANES), tok),
                  pl.BlockSpec((1, tm, d), tok),
                  pl.BlockSpec((1, 1, d), row),
                  _const_spec(w["dww"].shape), _const_spec(w["dwb"].shape),
                  _const_spec(w["cg"].shape), _const_spec(w["cb"].shape),
                  _const_spec(w["wo"].shape), _const_spec(w["bo"].shape),
                  _const_spec(w["l1g"].shape), _const_spec(w["l1b"].shape)],
        out_specs=pl.BlockSpec((1, tm, d), tok),
        out_shape=jax.ShapeDtypeStruct((b, n, d), F32),
        scratch_shapes=[pltpu.VMEM((C_CONV // LANES, tm + 2 * HALO, LANES), F32),
                        pltpu.VMEM((tm, C_CONV), F32)],
        compiler_params=_params("parallel", "parallel"),
        name="mix",
    )(glu, glu, glu, attn, x, g1, w["dww"], w["dwb"], w["cg"], w["cb"], w["wo"], w["bo"],
      w["l1g"], w["l1b"])


def _ffn_kernel(xp_ref, x_ref, xn_ref, sc_ref, sh_ref, g2_ref, wup_ref, fw_ref, fb_ref, wdn_ref,
                bdn_ref, lg_ref, lb_ref, o_ref, h_ref, u_ref, act_ref):
    tm = x_ref.shape[1]
    has_prev, has_next = _halo_masks()
    scale = 1.0 + sc_ref[0]
    shift = sh_ref[0]
    h_ref[...] = jnp.concatenate([(xp_ref[0] * scale + shift) * has_prev,
                                  x_ref[0] * scale + shift,
                                  (xn_ref[0] * scale + shift) * has_next], axis=0).astype(BF16)

    groups = FF_CHUNK // LANES

    def conv(slot, col0, u):
        outs = []
        for j in range(groups):
            u_ref[slot + j] = u[:, j * LANES:(j + 1) * LANES]
            cols = slice(col0 + j * LANES, col0 + (j + 1) * LANES)
            acc = fb_ref[:, cols]
            for kk in range(FFN_K):
                acc = acc + u_ref[slot + j, pl.ds(FFN_HALO - 1 + kk, tm), :] * fw_ref[kk:kk + 1, cols]
            outs.append(acc)
        return outs

    for c in range(D_FF // FF_CHUNK):
        g = conv(4 * (c % 2), c * FF_CHUNK, _dot(h_ref[...], wup_ref[:, c * FF_CHUNK:(c + 1) * FF_CHUNK]))
        v = conv(4 * (c % 2) + 2, D_FF + c * FF_CHUNK,
                 _dot(h_ref[...], wup_ref[:, D_FF + c * FF_CHUNK:D_FF + (c + 1) * FF_CHUNK]))
        for j in range(groups):
            cols = slice(c * FF_CHUNK + j * LANES, c * FF_CHUNK + (j + 1) * LANES)
            act_ref[:, cols] = (_silu(g[j]) * v[j]).astype(BF16)

    for r0 in range(0, tm, tm // 2):
        rows = slice(r0, r0 + tm // 2)
        f = _dot(act_ref[rows, :], wdn_ref[...]) + bdn_ref[...]
        o_ref[0, rows] = _norm(ALPHA * x_ref[0, rows] + g2_ref[0] * f) * lg_ref[...] + lb_ref[...]


def _ffn_call(xm, sc, sh, g2, w, tm):
    b, n, d = xm.shape
    tok = lambda bi, i: (bi, i, 0)
    row = lambda bi, i: (bi, 0, 0)
    return pl.pallas_call(
        _ffn_kernel,
        grid=(b, n // tm),
        in_specs=[*_halo_specs(tm, n, d, FFN_HALO),
                  pl.BlockSpec((1, 1, d), row), pl.BlockSpec((1, 1, d), row), pl.BlockSpec((1, 1, d), row),
                  _const_spec(w["wup"].shape), _const_spec(w["fw"].shape), _const_spec(w["fb"].shape),
                  _const_spec(w["wdn"].shape), _const_spec(w["bdn"].shape),
                  _const_spec(w["l2g"].shape), _const_spec(w["l2b"].shape)],
        out_specs=pl.BlockSpec((1, tm, d), tok),
        out_shape=jax.ShapeDtypeStruct((b, n, d), F32),
        scratch_shapes=[pltpu.VMEM((tm + 2 * FFN_HALO, d), BF16),
                        pltpu.VMEM((4 * FF_CHUNK // LANES, tm + 2 * FFN_HALO, LANES), F32),
                        pltpu.VMEM((tm, D_FF), BF16)],
        compiler_params=_params("parallel", "parallel"),
        name="ffn",
    )(xm, xm, xm, sc, sh, g2, w["wup"], w["fw"], w["fb"], w["wdn"], w["bdn"], w["l2g"], w["l2b"])


def _rope_tables(n):
    t = jnp.arange(n)
    n_freq = QK_ROPE // 4
    inv = ROPE_THETA ** (-jnp.arange(n_freq, dtype=F32) / n_freq)
    ang = jnp.concatenate([(t // GRID_W)[:, None] * inv, (t % GRID_W)[:, None] * inv], axis=-1)
    cos, sin = jnp.cos(ang), jnp.sin(ang)
    pad = jnp.zeros((n, LANES - QK_NOPE - QK_ROPE), F32)
    cos_t = jnp.concatenate([jnp.ones((n, QK_NOPE), F32), cos, cos, pad], axis=-1)
    sin_t = jnp.concatenate([jnp.zeros((n, QK_NOPE), F32), sin, sin, pad], axis=-1)
    return cos_t, sin_t


def _partner(w_rope):
    half = w_rope.shape[-1] // 2
    return jnp.concatenate([-w_rope[..., half:], w_rope[..., :half]], axis=-1)


def _layout_weights(w_in, w_uq, w_ukv):
    d = w_in.shape[0]
    tail = LANES - QK_NOPE - QK_ROPE
    w_in = w_in.astype(BF16)
    z = lambda r, c: jnp.zeros((r, c), BF16)
    kr = w_in[:, _Z_KR:_Z_KR + QK_ROPE]
    win = jnp.concatenate([w_in[:, :_Z_KR], z(d, QK_NOPE), kr, _partner(kr)], axis=-1)
    win_ctx = jnp.concatenate([w_in[:, _Z_CKV:_Z_KR], z(d, QK_NOPE), kr, z(d, tail)], axis=-1)

    wq = w_uq.reshape(Q_LORA, N_HEADS, QK_NOPE + QK_ROPE)
    wq = jnp.concatenate([wq, _partner(wq[..., QK_NOPE:])], axis=-1).reshape(Q_LORA, N_HEADS * LANES)
    wkv = w_ukv.reshape(KV_LORA, N_HEADS, QK_NOPE + V_DIM)
    wk = wkv[..., :QK_NOPE].reshape(KV_LORA, N_HEADS * QK_NOPE)
    wv = wkv[..., QK_NOPE:].reshape(KV_LORA, N_HEADS * V_DIM)
    cast = lambda a: a.astype(BF16)
    return dict(win=cast(win), win_ctx=cast(win_ctx), wq=cast(wq), wk=cast(wk), wv=cast(wv))


def kernel(x, c, ctx, c_ctx, w_ada, b_ada, w_in, conv_dw_w, conv_dw_b, conv_ln_g, conv_ln_b, q_norm_g,
           w_uq, kv_norm_g, w_ukv, w_o, b_o, ln1_g, ln1_b, w_up, ffn_dw_w, ffn_dw_b, w_down, b_down,
           ln2_g, ln2_b):
    b, n, d = x.shape
    tm = 512
    tq = 8 * ATTN_ROWS
    assert DEPTH == 1 and w_ada.shape[0] == 1
    assert n % tm == 0 and n % tq == 0 and tm % HALO == 0

    c_rows = jnp.concatenate([c, c_ctx[None, :], jnp.zeros((16 - b - 1, d), F32)], axis=0)
    mod = _mod_call(c_rows, w_ada[0], b_ada).reshape(16, 6, 1, d)
    sh1, sc1, g1, sh2, sc2, g2 = (mod[:b, j] for j in range(6))
    sh1c, sc1c = mod[b:b + 1, 0], mod[b:b + 1, 1]

    row2 = lambda a: a.reshape(1, -1)
    w = _layout_weights(w_in[0], w_uq[0], w_ukv[0])
    w.update(gq=row2(q_norm_g[0]), gkv=row2(kv_norm_g[0]))
    cos_t, sin_t = _rope_tables(n)

    glu, q, k_lat, v_lat = _proj_call(x, sc1, sh1, cos_t, sin_t, w, 2 * PROJ_ROWS)
    k_ctx, v_ctx = _proj_ctx_call(ctx, sc1c, sh1c, w)
    attn = _attn_call(q, k_lat, v_lat, k_ctx, v_ctx, tq)

    wm = dict(dww=conv_dw_w[0], dwb=row2(conv_dw_b[0]), cg=row2(conv_ln_g[0]), cb=row2(conv_ln_b[0]),
              wo=w_o[0].astype(BF16), bo=row2(b_o[0]), l1g=row2(ln1_g[0]), l1b=row2(ln1_b[0]))
    x_mid = _mix_call(glu, attn, x, g1, wm, 2 * tm)

    wf = dict(wup=w_up[0].astype(BF16), fw=ffn_dw_w[0], fb=row2(ffn_dw_b[0]), wdn=w_down[0].astype(BF16),
              bdn=row2(b_down[0]), l2g=row2(ln2_g[0]), l2b=row2(ln2_b[0]))
    return _ffn_call(x_mid, sc2, sh2, g2, wf, tm)
```

```python
import math

import jax
import jax.numpy as jnp
from jax import lax
from jax.experimental import pallas as pl
from jax.experimental.pallas import tpu as pltpu

F32 = jnp.float32
BF16 = jnp.bfloat16

D_MODEL = 1024
GRID_W = 64
C_CONV = 512
CONV_K = 31
N_HEADS = 8
QK_NOPE = 64
QK_ROPE = 32
V_DIM = 64
Q_LORA = 384
KV_LORA = 256
D_FF = 2816
FFN_K = 3
ROPE_THETA = 10000.0
LN_EPS = 1e-5
DEPTH = 1
ALPHA = (2 * DEPTH) ** 0.25
SOFTMAX_SCALE = (QK_NOPE + QK_ROPE) ** -0.5
LOG2E = math.log2(math.e)

LANES = 128
SUBLANES = 8
HEAD_PAIRS = N_HEADS // 2
V_TILE = 2 * LANES
HALO = 16
FFN_HALO = SUBLANES
FF_CHUNK = 256
KEY_CHUNK = 256
ATTN_ROWS = 512
PROJ_ROWS = 512
VMEM_LIMIT = 52 * 1024 * 1024

_Z_CQ = 2 * C_CONV
_Z_CKV = _Z_CQ + Q_LORA
_Z_KR = _Z_CKV + KV_LORA
_Z_END = _Z_KR + LANES
PARTNER_ROLL = LANES - QK_ROPE


def _sigmoid(x):
    return 0.5 * jnp.tanh(0.5 * x) + 0.5


def _silu(x):
    h = 0.5 * x
    return h * jnp.tanh(h) + h


def _norm(x, gain=1.0):
    mu = jnp.mean(x, axis=-1, keepdims=True)
    xc = x - mu
    var = jnp.mean(xc * xc, axis=-1, keepdims=True)
    return xc * (lax.rsqrt(var + LN_EPS) * gain)


def _rms(x, g):
    return x * lax.rsqrt(jnp.mean(x * x, axis=-1, keepdims=True) + LN_EPS) * g


def _dot(a, b):
    return jnp.dot(a, b, preferred_element_type=F32)


def _params(*sem):
    return pltpu.CompilerParams(dimension_semantics=sem, vmem_limit_bytes=VMEM_LIMIT)


def _const_spec(shape):
    zeros = (0,) * len(shape)
    return pl.BlockSpec(shape, lambda *_: zeros)


def _mod_kernel(c_ref, w_ref, b_ref, o_ref):
    c = c_ref[...]
    a = _silu(c)
    o_ref[...] = jnp.dot(a, w_ref[...], precision=lax.Precision.HIGHEST,
                         preferred_element_type=F32) + b_ref[...]


def _mod_call(c_rows, w_ada, b_ada):
    rows, d = c_rows.shape
    n = w_ada.shape[1]
    bn = 1536
    return pl.pallas_call(
        _mod_kernel,
        grid=(n // bn,),
        in_specs=[pl.BlockSpec((rows, d), lambda j: (0, 0)),
                  pl.BlockSpec((d, bn), lambda j: (0, j)),
                  pl.BlockSpec((1, bn), lambda j: (0, j))],
        out_specs=pl.BlockSpec((rows, bn), lambda j: (0, j)),
        out_shape=jax.ShapeDtypeStruct((rows, n), F32),
        compiler_params=_params("arbitrary"),
        name="mod",
    )(c_rows, w_ada, b_ada)


def _store_kv(ckv, kr128, gkv_ref, wk_ref, wv_ref, k_ref, v_ref, rows):
    ckvn = _rms(ckv, gkv_ref[...]).astype(BF16)
    kk = _dot(ckvn, wk_ref[...])
    lane = lax.broadcasted_iota(jnp.int32, kr128.shape, 1)
    for h in range(N_HEADS):
        pair = kk[:, (h // 2) * LANES:(h // 2 + 1) * LANES]
        if h % 2:
            pair = pltpu.roll(pair, QK_NOPE, 1)
        k_ref[0, h, rows] = jnp.where(lane < QK_NOPE, pair, kr128).astype(BF16)
    vv = _dot(ckvn, wv_ref[...])
    one_col = (lax.broadcasted_iota(jnp.int32, (vv.shape[0], LANES), 1) == 0).astype(BF16)
    for p in range(HEAD_PAIRS):
        v_ref[0, p, rows, :LANES] = vv[:, p * LANES:(p + 1) * LANES].astype(BF16)
        v_ref[0, p, rows, LANES:] = one_col


def _rotary(t, cos, sin):
    return t * cos + pltpu.roll(t, PARTNER_ROLL, 1) * sin


def _proj_kernel(x_ref, sc_ref, sh_ref, cos_ref, sin_ref, win_ref, gq_ref, wq_ref,
                 gkv_ref, wk_ref, wv_ref, glu_ref, q_ref, k_ref, v_ref):
    for r0 in range(0, x_ref.shape[1], PROJ_ROWS):
        rows = slice(r0, r0 + PROJ_ROWS)
        h = _norm(x_ref[0, rows]) * (1.0 + sc_ref[0]) + sh_ref[0]
        z = _dot(h.astype(BF16), win_ref[...])
        glu_ref[0, rows] = z[:, :C_CONV] * _sigmoid(z[:, C_CONV:_Z_CQ])
        cos = cos_ref[rows]
        sin = sin_ref[rows]
        cqn = _rms(z[:, _Z_CQ:_Z_CKV], gq_ref[...]).astype(BF16)
        qa = _dot(cqn, wq_ref[...])
        for hd in range(N_HEADS):
            q = _rotary(qa[:, hd * LANES:(hd + 1) * LANES], cos, sin) * (SOFTMAX_SCALE * LOG2E)
            q_ref[0, hd, rows] = q.astype(BF16)
        kr128 = _rotary(z[:, _Z_KR:_Z_END], cos, sin)
        _store_kv(z[:, _Z_CKV:_Z_KR], kr128, gkv_ref, wk_ref, wv_ref, k_ref, v_ref, rows)


def _proj_ctx_kernel(x_ref, sc_ref, sh_ref, win_ref, gkv_ref, wk_ref, wv_ref, k_ref, v_ref):
    h = _norm(x_ref[0]) * (1.0 + sc_ref[0]) + sh_ref[0]
    z = _dot(h.astype(BF16), win_ref[...])
    _store_kv(z[:, :KV_LORA], z[:, KV_LORA:], gkv_ref, wk_ref, wv_ref, k_ref, v_ref, slice(None))


def _proj_call(x, sc, sh, cos_t, sin_t, w, tm):
    b, n, d = x.shape
    nt = n // tm
    tok = lambda bi, i: (bi, i, 0)
    row = lambda bi, i: (bi, 0, 0)
    tab = lambda bi, i: (i, 0)
    hd4 = lambda bi, i: (bi, 0, i, 0)
    return pl.pallas_call(
        _proj_kernel,
        grid=(b, nt),
        in_specs=[pl.BlockSpec((1, tm, d), tok),
                  pl.BlockSpec((1, 1, d), row), pl.BlockSpec((1, 1, d), row),
                  pl.BlockSpec((tm, LANES), tab), pl.BlockSpec((tm, LANES), tab),
                  _const_spec(w["win"].shape), _const_spec(w["gq"].shape), _const_spec(w["wq"].shape),
                  _const_spec(w["gkv"].shape), _const_spec(w["wk"].shape), _const_spec(w["wv"].shape)],
        out_specs=[pl.BlockSpec((1, tm, C_CONV), tok),
                   pl.BlockSpec((1, N_HEADS, tm, LANES), hd4),
                   pl.BlockSpec((1, N_HEADS, tm, LANES), hd4),
                   pl.BlockSpec((1, HEAD_PAIRS, tm, V_TILE), hd4)],
        out_shape=[jax.ShapeDtypeStruct((b, n, C_CONV), F32),
                   jax.ShapeDtypeStruct((b, N_HEADS, n, LANES), BF16),
                   jax.ShapeDtypeStruct((b, N_HEADS, n, LANES), BF16),
                   jax.ShapeDtypeStruct((b, HEAD_PAIRS, n, V_TILE), BF16)],
        compiler_params=_params("parallel", "parallel"),
        name="proj",
    )(x, sc, sh, cos_t, sin_t, w["win"], w["gq"], w["wq"], w["gkv"], w["wk"], w["wv"])


def _proj_ctx_call(ctx, sc, sh, w):
    b, n_ctx, d = ctx.shape
    row = lambda bi: (0, 0, 0)
    kv = lambda bi: (bi, 0, 0, 0)
    return pl.pallas_call(
        _proj_ctx_kernel,
        grid=(b,),
        in_specs=[pl.BlockSpec((1, n_ctx, d), lambda bi: (bi, 0, 0)),
                  pl.BlockSpec((1, 1, d), row), pl.BlockSpec((1, 1, d), row),
                  _const_spec(w["win_ctx"].shape), _const_spec(w["gkv"].shape),
                  _const_spec(w["wk"].shape), _const_spec(w["wv"].shape)],
        out_specs=[pl.BlockSpec((1, N_HEADS, n_ctx, LANES), kv),
                   pl.BlockSpec((1, HEAD_PAIRS, n_ctx, V_TILE), kv)],
        out_shape=[jax.ShapeDtypeStruct((b, N_HEADS, n_ctx, LANES), BF16),
                   jax.ShapeDtypeStruct((b, HEAD_PAIRS, n_ctx, V_TILE), BF16)],
        compiler_params=_params("parallel"),
        name="proj_ctx",
    )(ctx, sc, sh, w["win_ctx"], w["gkv"], w["wk"], w["wv"])


def _attn_kernel(q_ref, k_ref, v_ref, kc_ref, vc_ref, o_ref):
    chunks = [(k_ref, v_ref, c0) for c0 in range(0, k_ref.shape[2], KEY_CHUNK)]
    chunks += [(kc_ref, vc_ref, c0) for c0 in range(0, kc_ref.shape[2], KEY_CHUNK)]
    n_chunks = len(chunks)
    n_sub = q_ref.shape[2] // ATTN_ROWS

    def scores(unit, c):
        sub, j = unit
        keys, _, c0 = chunks[c]
        return lax.dot_general(q_ref[0, j, sub * ATTN_ROWS:(sub + 1) * ATTN_ROWS],
                               keys[0, j, c0:c0 + KEY_CHUNK],
                               (((1,), (1,)), ((), ())), preferred_element_type=F32)

    def row_max(s):
        m = s[0]
        for sc in s[1:]:
            m = jnp.maximum(m, sc)
        return jnp.max(m, axis=-1, keepdims=True)

    def weighted_values(s, m, c, acc):
        _, values, c0 = chunks[c]
        pv = _dot(jnp.exp2(s[c] - m).astype(BF16), values[0, 0, c0:c0 + KEY_CHUNK])
        return pv if acc is None else acc + pv

    units = [(sub, j) for sub in range(n_sub) for j in range(2)]
    s_cur = [scores(units[0], c) for c in range(n_chunks)]
    outs = []
    for u in range(len(units)):
        m_cur, acc, s_nxt = row_max(s_cur), None, []
        for c in range(n_chunks):
            if u + 1 < len(units):
                s_nxt.append(scores(units[u + 1], c))
            acc = weighted_values(s_cur, m_cur, c, acc)
        outs.append(acc[:, :LANES] * (1.0 / acc[:, LANES:LANES + 1]))
        s_cur = s_nxt
    lane = lax.broadcasted_iota(jnp.int32, outs[0].shape, 1)
    for sub in range(n_sub):
        o_ref[0, sub * ATTN_ROWS:(sub + 1) * ATTN_ROWS] = jnp.where(
            lane < V_DIM, outs[2 * sub], outs[2 * sub + 1]).astype(BF16)


def _attn_call(q, k, v, kc, vc, tq):
    b, _, n, _ = q.shape
    nk, nc = k.shape[2], kc.shape[2]
    assert nk % KEY_CHUNK == 0 and nc % KEY_CHUNK == 0
    return pl.pallas_call(
        _attn_kernel,
        grid=(b, HEAD_PAIRS, n // tq),
        in_specs=[pl.BlockSpec((1, 2, tq, LANES), lambda bi, p, i: (bi, p, i, 0)),
                  pl.BlockSpec((1, 2, nk, LANES), lambda bi, p, i: (bi, p, 0, 0)),
                  pl.BlockSpec((1, 1, nk, V_TILE), lambda bi, p, i: (bi, p, 0, 0)),
                  pl.BlockSpec((1, 2, nc, LANES), lambda bi, p, i: (bi, p, 0, 0)),
                  pl.BlockSpec((1, 1, nc, V_TILE), lambda bi, p, i: (bi, p, 0, 0))],
        out_specs=pl.BlockSpec((1, tq, LANES), lambda bi, p, i: (bi, i, p)),
        out_shape=jax.ShapeDtypeStruct((b, n, HEAD_PAIRS * LANES), BF16),
        compiler_params=_params("parallel", "parallel", "arbitrary"),
        name="attn",
    )(q, k, v, kc, vc)


def _halo_specs(tm, n, width, halo):
    per = tm // halo
    last = n // halo - 1
    return (pl.BlockSpec((1, halo, width), lambda bi, i: (bi, jnp.maximum(i * per - 1, 0), 0)),
            pl.BlockSpec((1, tm, width), lambda bi, i: (bi, i, 0)),
            pl.BlockSpec((1, halo, width), lambda bi, i: (bi, jnp.minimum((i + 1) * per, last), 0)))


def _halo_masks():
    i = pl.program_id(1)
    has_prev = (i > 0).astype(F32)
    has_next = (i < pl.num_programs(1) - 1).astype(F32)
    return has_prev, has_next


CONV_ROWS = 128


def _mix_kernel(gp_ref, g_ref, gn_ref, a_ref, x_ref, g1_ref, dww_ref, dwb_ref, cg_ref, cb_ref,
                wo_ref, bo_ref, lg_ref, lb_ref, o_ref, win_ref, conv_ref):
    tm = g_ref.shape[1]
    has_prev, has_next = _halo_masks()
    for j in range(C_CONV // LANES):
        cols = slice(j * LANES, (j + 1) * LANES)
        win_ref[j, 0:HALO] = gp_ref[0, :, cols] * has_prev
        win_ref[j, HALO:HALO + tm] = g_ref[0, :, cols]
        win_ref[j, HALO + tm:] = gn_ref[0, :, cols] * has_next
    half = CONV_K // 2

    def rows(c, carry):
        base = pl.multiple_of(c * CONV_ROWS, CONV_ROWS)
        for j in range(C_CONV // LANES):
            cols = slice(j * LANES, (j + 1) * LANES)
            acc = jnp.zeros((CONV_ROWS, LANES), F32) + dwb_ref[:, cols]
            for kk in range(CONV_K):
                acc = acc + (win_ref[j, pl.ds(base + (HALO - half + kk), CONV_ROWS), :]
                             * dww_ref[kk:kk + 1, cols])
            conv_ref[pl.ds(base, CONV_ROWS), cols] = acc
        return carry

    lax.fori_loop(0, tm // CONV_ROWS, rows, 0)
    c = _silu(_norm(conv_ref[...]) * cg_ref[...] + cb_ref[...]).astype(BF16)
    y = _dot(c, wo_ref[:C_CONV, :]) + _dot(a_ref[0], wo_ref[C_CONV:, :]) + bo_ref[...]
    o_ref[0] = _norm(_norm(x_ref[0], ALPHA) + g1_ref[0] * y) * lg_ref[...] + lb_ref[...]


def _mix_call(glu, attn, x, g1, w, tm):
    b, n, d = x.shape
    tok = lambda bi, i: (bi, i, 0)
    row = lambda bi, i: (bi, 0, 0)
    return pl.pallas_call(
        _mix_kernel,
        grid=(b, n // tm),
        in_specs=[*_halo_specs(tm, n, C_CONV, HALO),
                  pl.BlockSpec((1, tm, HEAD_PAIRS * LANES), tok),
                  pl.BlockSpec((1, tm, d), tok),
                  pl.BlockSpec((1, 1, d), row),
                  _const_spec(w["dww"].shape), _const_spec(w["dwb"].shape),
                  _const_spec(w["cg"].shape), _const_spec(w["cb"].shape),
                  _const_spec(w["wo"].shape), _const_spec(w["bo"].shape),
                  _const_spec(w["l1g"].shape), _const_spec(w["l1b"].shape)],
        out_specs=pl.BlockSpec((1, tm, d), tok),
        out_shape=jax.ShapeDtypeStruct((b, n, d), F32),
        scratch_shapes=[pltpu.VMEM((C_CONV // LANES, tm + 2 * HALO, LANES), F32),
                        pltpu.VMEM((tm, C_CONV), F32)],
        compiler_params=_params("parallel", "parallel"),
        name="mix",
    )(glu, glu, glu, attn, x, g1, w["dww"], w["dwb"], w["cg"], w["cb"], w["wo"], w["bo"],
      w["l1g"], w["l1b"])


def _ffn_kernel(xp_ref, x_ref, xn_ref, sc_ref, sh_ref, g2_ref, wup_ref, fw_ref, fb_ref, wdn_ref,
                bdn_ref, lg_ref, lb_ref, o_ref, h_ref, u_ref, act_ref):
    tm = x_ref.shape[1]
    has_prev, has_next = _halo_masks()
    scale = 1.0 + sc_ref[0]
    shift = sh_ref[0]
    h_ref[...] = jnp.concatenate([(xp_ref[0] * scale + shift) * has_prev,
                                  x_ref[0] * scale + shift,
                                  (xn_ref[0] * scale + shift) * has_next], axis=0).astype(BF16)

    groups = FF_CHUNK // LANES

    def conv(slot, col0, u):
        outs = []
        for j in range(groups):
            u_ref[slot + j] = u[:, j * LANES:(j + 1) * LANES]
            cols = slice(col0 + j * LANES, col0 + (j + 1) * LANES)
            acc = fb_ref[:, cols]
            for kk in range(FFN_K):
                acc = acc + u_ref[slot + j, pl.ds(FFN_HALO - 1 + kk, tm), :] * fw_ref[kk:kk + 1, cols]
            outs.append(acc)
        return outs

    for c in range(D_FF // FF_CHUNK):
        g = conv(4 * (c % 2), c * FF_CHUNK, _dot(h_ref[...], wup_ref[:, c * FF_CHUNK:(c + 1) * FF_CHUNK]))
        v = conv(4 * (c % 2) + 2, D_FF + c * FF_CHUNK,
                 _dot(h_ref[...], wup_ref[:, D_FF + c * FF_CHUNK:D_FF + (c + 1) * FF_CHUNK]))
        for j in range(groups):
            cols = slice(c * FF_CHUNK + j * LANES, c * FF_CHUNK + (j + 1) * LANES)
            act_ref[:, cols] = (_silu(g[j]) * v[j]).astype(BF16)

    for r0 in range(0, tm, tm // 2):
        rows = slice(r0, r0 + tm // 2)
        f = _dot(act_ref[rows, :], wdn_ref[...]) + bdn_ref[...]
        o_ref[0, rows] = _norm(ALPHA * x_ref[0, rows] + g2_ref[0] * f) * lg_ref[...] + lb_ref[...]


def _ffn_call(xm, sc, sh, g2, w, tm):
    b, n, d = xm.shape
    tok = lambda bi, i: (bi, i, 0)
    row = lambda bi, i: (bi, 0, 0)
    return pl.pallas_call(
        _ffn_kernel,
        grid=(b, n // tm),
        in_specs=[*_halo_specs(tm, n, d, FFN_HALO),
                  pl.BlockSpec((1, 1, d), row), pl.BlockSpec((1, 1, d), row), pl.BlockSpec((1, 1, d), row),
                  _const_spec(w["wup"].shape), _const_spec(w["fw"].shape), _const_spec(w["fb"].shape),
                  _const_spec(w["wdn"].shape), _const_spec(w["bdn"].shape),
                  _const_spec(w["l2g"].shape), _const_spec(w["l2b"].shape)],
        out_specs=pl.BlockSpec((1, tm, d), tok),
        out_shape=jax.ShapeDtypeStruct((b, n, d), F32),
        scratch_shapes=[pltpu.VMEM((tm + 2 * FFN_HALO, d), BF16),
                        pltpu.VMEM((4 * FF_CHUNK // LANES, tm + 2 * FFN_HALO, LANES), F32),
                        pltpu.VMEM((tm, D_FF), BF16)],
        compiler_params=_params("parallel", "parallel"),
        name="ffn",
    )(xm, xm, xm, sc, sh, g2, w["wup"], w["fw"], w["fb"], w["wdn"], w["bdn"], w["l2g"], w["l2b"])


def _rope_tables(n):
    t = jnp.arange(n)
    n_freq = QK_ROPE // 4
    inv = ROPE_THETA ** (-jnp.arange(n_freq, dtype=F32) / n_freq)
    ang = jnp.concatenate([(t // GRID_W)[:, None] * inv, (t % GRID_W)[:, None] * inv], axis=-1)
    cos, sin = jnp.cos(ang), jnp.sin(ang)
    pad = jnp.zeros((n, LANES - QK_NOPE - QK_ROPE), F32)
    cos_t = jnp.concatenate([jnp.ones((n, QK_NOPE), F32), cos, cos, pad], axis=-1)
    sin_t = jnp.concatenate([jnp.zeros((n, QK_NOPE), F32), sin, sin, pad], axis=-1)
    return cos_t, sin_t


def _partner(w_rope):
    half = w_rope.shape[-1] // 2
    return jnp.concatenate([-w_rope[..., half:], w_rope[..., :half]], axis=-1)


def _layout_weights(w_in, w_uq, w_ukv):
    d = w_in.shape[0]
    tail = LANES - QK_NOPE - QK_ROPE
    w_in = w_in.astype(BF16)
    z = lambda r, c: jnp.zeros((r, c), BF16)
    kr = w_in[:, _Z_KR:_Z_KR + QK_ROPE]
    win = jnp.concatenate([w_in[:, :_Z_KR], z(d, QK_NOPE), kr, _partner(kr)], axis=-1)
    win_ctx = jnp.concatenate([w_in[:, _Z_CKV:_Z_KR], z(d, QK_NOPE), kr, z(d, tail)], axis=-1)

    wq = w_uq.reshape(Q_LORA, N_HEADS, QK_NOPE + QK_ROPE)
    wq = jnp.concatenate([wq, _partner(wq[..., QK_NOPE:])], axis=-1).reshape(Q_LORA, N_HEADS * LANES)
    wkv = w_ukv.reshape(KV_LORA, N_HEADS, QK_NOPE + V_DIM)
    wk = wkv[..., :QK_NOPE].reshape(KV_LORA, N_HEADS * QK_NOPE)
    wv = wkv[..., QK_NOPE:].reshape(KV_LORA, N_HEADS * V_DIM)
    cast = lambda a: a.astype(BF16)
    return dict(win=cast(win), win_ctx=cast(win_ctx), wq=cast(wq), wk=cast(wk), wv=cast(wv))


def kernel(x, c, ctx, c_ctx, w_ada, b_ada, w_in, conv_dw_w, conv_dw_b, conv_ln_g, conv_ln_b, q_norm_g,
           w_uq, kv_norm_g, w_ukv, w_o, b_o, ln1_g, ln1_b, w_up, ffn_dw_w, ffn_dw_b, w_down, b_down,
           ln2_g, ln2_b):
    b, n, d = x.shape
    tm = 512
    tq = 4 * ATTN_ROWS
    assert DEPTH == 1 and w_ada.shape[0] == 1
    assert n % tm == 0 and n % tq == 0 and tm % HALO == 0

    c_rows = jnp.concatenate([c, c_ctx[None, :], jnp.zeros((16 - b - 1, d), F32)], axis=0)
    mod = _mod_call(c_rows, w_ada[0], b_ada).reshape(16, 6, 1, d)
    sh1, sc1, g1, sh2, sc2, g2 = (mod[:b, j] for j in range(6))
    sh1c, sc1c = mod[b:b + 1, 0], mod[b:b + 1, 1]

    row2 = lambda a: a.reshape(1, -1)
    w = _layout_weights(w_in[0], w_uq[0], w_ukv[0])
    w.update(gq=row2(q_norm_g[0]), gkv=row2(kv_norm_g[0]))
    cos_t, sin_t = _rope_tables(n)

    glu, q, k_lat, v_lat = _proj_call(x, sc1, sh1, cos_t, sin_t, w, 2 * PROJ_ROWS)
    k_ctx, v_ctx = _proj_ctx_call(ctx, sc1c, sh1c, w)
    attn = _attn_call(q, k_lat, v_lat, k_ctx, v_ctx, tq)

    wm = dict(dww=conv_dw_w[0], dwb=row2(conv_dw_b[0]), cg=row2(conv_ln_g[0]), cb=row2(conv_ln_b[0]),
              wo=w_o[0].astype(BF16), bo=row2(b_o[0]), l1g=row2(ln1_g[0]), l1b=row2(ln1_b[0]))
    x_mid = _mix_call(glu, attn, x, g1, wm, 2 * tm)

    wf = dict(wup=w_up[0].astype(BF16), fw=ffn_dw_w[0], fb=row2(ffn_dw_b[0]), wdn=w_down[0].astype(BF16),
              bdn=row2(b_down[0]), l2g=row2(ln2_g[0]), l2b=row2(ln2_b[0]))
    return _ffn_call(x_mid, sc2, sh2, g2, wf, tm)
```

```python
import math

import jax
import jax.numpy as jnp
from jax import lax
from jax.experimental import pallas as pl
from jax.experimental.pallas import tpu as pltpu

F32 = jnp.float32
BF16 = jnp.bfloat16

D_MODEL = 1024
GRID_W = 64
C_CONV = 512
CONV_K = 31
N_HEADS = 8
QK_NOPE = 64
QK_ROPE = 32
V_DIM = 64
Q_LORA = 384
KV_LORA = 256
D_FF = 2816
FFN_K = 3
ROPE_THETA = 10000.0
LN_EPS = 1e-5
DEPTH = 1
ALPHA = (2 * DEPTH) ** 0.25
SOFTMAX_SCALE = (QK_NOPE + QK_ROPE) ** -0.5
LOG2E = math.log2(math.e)

LANES = 128
SUBLANES = 8
HEAD_PAIRS = N_HEADS // 2
V_TILE = 2 * LANES
HALO = 16
FFN_HALO = SUBLANES
FF_CHUNK = 256
KEY_CHUNK = 256
ATTN_ROWS = 512
PROJ_ROWS = 512
VMEM_LIMIT = 52 * 1024 * 1024

_Z_CQ = 2 * C_CONV
_Z_CKV = _Z_CQ + Q_LORA
_Z_KR = _Z_CKV + KV_LORA
_Z_END = _Z_KR + LANES
PARTNER_ROLL = LANES - QK_ROPE


def _sigmoid(x):
    return 0.5 * jnp.tanh(0.5 * x) + 0.5


def _silu(x):
    h = 0.5 * x
    return h * jnp.tanh(h) + h


def _norm(x, gain=1.0):
    mu = jnp.mean(x, axis=-1, keepdims=True)
    xc = x - mu
    var = jnp.mean(xc * xc, axis=-1, keepdims=True)
    return xc * (lax.rsqrt(var + LN_EPS) * gain)


def _rms(x, g):
    return x * lax.rsqrt(jnp.mean(x * x, axis=-1, keepdims=True) + LN_EPS) * g


def _dot(a, b):
    return jnp.dot(a, b, preferred_element_type=F32)


def _params(*sem):
    return pltpu.CompilerParams(dimension_semantics=sem, vmem_limit_bytes=VMEM_LIMIT)


def _const_spec(shape):
    zeros = (0,) * len(shape)
    return pl.BlockSpec(shape, lambda *_: zeros)


def _split(x):
    hi = x.astype(BF16)
    return hi, (x - hi.astype(F32)).astype(BF16)


def _mod_kernel(c_ref, w_ref, b_ref, o_ref):
    a_hi, a_lo = _split(_silu(c_ref[...]))
    w_hi, w_lo = _split(w_ref[...])
    o_ref[...] = _dot(a_hi, w_hi) + _dot(a_hi, w_lo) + _dot(a_lo, w_hi) + b_ref[...]


def _mod_call(c_rows, w_ada, b_ada):
    rows, d = c_rows.shape
    n = w_ada.shape[1]
    bn = 1536
    return pl.pallas_call(
        _mod_kernel,
        grid=(n // bn,),
        in_specs=[pl.BlockSpec((rows, d), lambda j: (0, 0)),
                  pl.BlockSpec((d, bn), lambda j: (0, j)),
                  pl.BlockSpec((1, bn), lambda j: (0, j))],
        out_specs=pl.BlockSpec((rows, bn), lambda j: (0, j)),
        out_shape=jax.ShapeDtypeStruct((rows, n), F32),
        compiler_params=_params("arbitrary"),
        name="mod",
    )(c_rows, w_ada, b_ada)


def _store_kv(ckv, kr128, gkv_ref, wk_ref, wv_ref, k_ref, v_ref, rows):
    ckvn = _rms(ckv, gkv_ref[...]).astype(BF16)
    kk = _dot(ckvn, wk_ref[...])
    lane = lax.broadcasted_iota(jnp.int32, kr128.shape, 1)
    for h in range(N_HEADS):
        pair = kk[:, (h // 2) * LANES:(h // 2 + 1) * LANES]
        if h % 2:
            pair = pltpu.roll(pair, QK_NOPE, 1)
        k_ref[0, h, rows] = jnp.where(lane < QK_NOPE, pair, kr128).astype(BF16)
    vv = _dot(ckvn, wv_ref[...])
    one_col = (lax.broadcasted_iota(jnp.int32, (vv.shape[0], LANES), 1) == 0).astype(BF16)
    for p in range(HEAD_PAIRS):
        v_ref[0, p, rows, :LANES] = vv[:, p * LANES:(p + 1) * LANES].astype(BF16)
        v_ref[0, p, rows, LANES:] = one_col


def _rotary(t, cos, sin):
    return t * cos + pltpu.roll(t, PARTNER_ROLL, 1) * sin


def _proj_kernel(x_ref, sc_ref, sh_ref, cos_ref, sin_ref, win_ref, gq_ref, wq_ref,
                 gkv_ref, wk_ref, wv_ref, glu_ref, q_ref, k_ref, v_ref):
    for r0 in range(0, x_ref.shape[1], PROJ_ROWS):
        rows = slice(r0, r0 + PROJ_ROWS)
        h = _norm(x_ref[0, rows]) * (1.0 + sc_ref[0]) + sh_ref[0]
        z = _dot(h.astype(BF16), win_ref[...])
        glu_ref[0, rows] = z[:, :C_CONV] * _sigmoid(z[:, C_CONV:_Z_CQ])
        cos = cos_ref[rows]
        sin = sin_ref[rows]
        cqn = _rms(z[:, _Z_CQ:_Z_CKV], gq_ref[...]).astype(BF16)
        qa = _dot(cqn, wq_ref[...])
        for hd in range(N_HEADS):
            q = _rotary(qa[:, hd * LANES:(hd + 1) * LANES], cos, sin) * (SOFTMAX_SCALE * LOG2E)
            q_ref[0, hd, rows] = q.astype(BF16)
        kr128 = _rotary(z[:, _Z_KR:_Z_END], cos, sin)
        _store_kv(z[:, _Z_CKV:_Z_KR], kr128, gkv_ref, wk_ref, wv_ref, k_ref, v_ref, rows)


def _proj_ctx_kernel(x_ref, sc_ref, sh_ref, win_ref, gkv_ref, wk_ref, wv_ref, k_ref, v_ref):
    h = _norm(x_ref[0]) * (1.0 + sc_ref[0]) + sh_ref[0]
    z = _dot(h.astype(BF16), win_ref[...])
    _store_kv(z[:, :KV_LORA], z[:, KV_LORA:], gkv_ref, wk_ref, wv_ref, k_ref, v_ref, slice(None))


def _proj_call(x, sc, sh, cos_t, sin_t, w, tm):
    b, n, d = x.shape
    nt = n // tm
    tok = lambda bi, i: (bi, i, 0)
    row = lambda bi, i: (bi, 0, 0)
    tab = lambda bi, i: (i, 0)
    hd4 = lambda bi, i: (bi, 0, i, 0)
    return pl.pallas_call(
        _proj_kernel,
        grid=(b, nt),
        in_specs=[pl.BlockSpec((1, tm, d), tok),
                  pl.BlockSpec((1, 1, d), row), pl.BlockSpec((1, 1, d), row),
                  pl.BlockSpec((tm, LANES), tab), pl.BlockSpec((tm, LANES), tab),
                  _const_spec(w["win"].shape), _const_spec(w["gq"].shape), _const_spec(w["wq"].shape),
                  _const_spec(w["gkv"].shape), _const_spec(w["wk"].shape), _const_spec(w["wv"].shape)],
        out_specs=[pl.BlockSpec((1, tm, C_CONV), tok),
                   pl.BlockSpec((1, N_HEADS, tm, LANES), hd4),
                   pl.BlockSpec((1, N_HEADS, tm, LANES), hd4),
                   pl.BlockSpec((1, HEAD_PAIRS, tm, V_TILE), hd4)],
        out_shape=[jax.ShapeDtypeStruct((b, n, C_CONV), F32),
                   jax.ShapeDtypeStruct((b, N_HEADS, n, LANES), BF16),
                   jax.ShapeDtypeStruct((b, N_HEADS, n, LANES), BF16),
                   jax.ShapeDtypeStruct((b, HEAD_PAIRS, n, V_TILE), BF16)],
        compiler_params=_params("parallel", "parallel"),
        name="proj",
    )(x, sc, sh, cos_t, sin_t, w["win"], w["gq"], w["wq"], w["gkv"], w["wk"], w["wv"])


def _proj_ctx_call(ctx, sc, sh, w):
    b, n_ctx, d = ctx.shape
    row = lambda bi: (0, 0, 0)
    kv = lambda bi: (bi, 0, 0, 0)
    return pl.pallas_call(
        _proj_ctx_kernel,
        grid=(b,),
        in_specs=[pl.BlockSpec((1, n_ctx, d), lambda bi: (bi, 0, 0)),
                  pl.BlockSpec((1, 1, d), row), pl.BlockSpec((1, 1, d), row),
                  _const_spec(w["win_ctx"].shape), _const_spec(w["gkv"].shape),
                  _const_spec(w["wk"].shape), _const_spec(w["wv"].shape)],
        out_specs=[pl.BlockSpec((1, N_HEADS, n_ctx, LANES), kv),
                   pl.BlockSpec((1, HEAD_PAIRS, n_ctx, V_TILE), kv)],
        out_shape=[jax.ShapeDtypeStruct((b, N_HEADS, n_ctx, LANES), BF16),
                   jax.ShapeDtypeStruct((b, HEAD_PAIRS, n_ctx, V_TILE), BF16)],
        compiler_params=_params("parallel"),
        name="proj_ctx",
    )(ctx, sc, sh, w["win_ctx"], w["gkv"], w["wk"], w["wv"])


def _attn_kernel(q_ref, k_ref, v_ref, kc_ref, vc_ref, o_ref):
    chunks = [(k_ref, v_ref, c0) for c0 in range(0, k_ref.shape[2], KEY_CHUNK)]
    chunks += [(kc_ref, vc_ref, c0) for c0 in range(0, kc_ref.shape[2], KEY_CHUNK)]
    n_chunks = len(chunks)
    n_sub = q_ref.shape[2] // ATTN_ROWS

    def scores(unit, c):
        sub, j = unit
        keys, _, c0 = chunks[c]
        return lax.dot_general(q_ref[0, j, sub * ATTN_ROWS:(sub + 1) * ATTN_ROWS],
                               keys[0, j, c0:c0 + KEY_CHUNK],
                               (((1,), (1,)), ((), ())), preferred_element_type=F32)

    def row_max(s):
        m = s[0]
        for sc in s[1:]:
            m = jnp.maximum(m, sc)
        return jnp.max(m, axis=-1, keepdims=True)

    def weighted_values(s, m, c, acc):
        _, values, c0 = chunks[c]
        pv = _dot(jnp.exp2(s[c] - m).astype(BF16), values[0, 0, c0:c0 + KEY_CHUNK])
        return pv if acc is None else acc + pv

    units = [(sub, j) for sub in range(n_sub) for j in range(2)]
    s_cur = [scores(units[0], c) for c in range(n_chunks)]
    outs = []
    for u in range(len(units)):
        m_cur, acc, s_nxt = row_max(s_cur), None, []
        for c in range(n_chunks):
            if u + 1 < len(units):
                s_nxt.append(scores(units[u + 1], c))
            acc = weighted_values(s_cur, m_cur, c, acc)
        outs.append(acc[:, :LANES] * (1.0 / acc[:, LANES:LANES + 1]))
        s_cur = s_nxt
    lane = lax.broadcasted_iota(jnp.int32, outs[0].shape, 1)
    for sub in range(n_sub):
        o_ref[0, sub * ATTN_ROWS:(sub + 1) * ATTN_ROWS] = jnp.where(
            lane < V_DIM, outs[2 * sub], outs[2 * sub + 1]).astype(BF16)


def _attn_call(q, k, v, kc, vc, tq):
    b, _, n, _ = q.shape
    nk, nc = k.shape[2], kc.shape[2]
    assert nk % KEY_CHUNK == 0 and nc % KEY_CHUNK == 0
    return pl.pallas_call(
        _attn_kernel,
        grid=(b, HEAD_PAIRS, n // tq),
        in_specs=[pl.BlockSpec((1, 2, tq, LANES), lambda bi, p, i: (bi, p, i, 0)),
                  pl.BlockSpec((1, 2, nk, LANES), lambda bi, p, i: (bi, p, 0, 0)),
                  pl.BlockSpec((1, 1, nk, V_TILE), lambda bi, p, i: (bi, p, 0, 0)),
                  pl.BlockSpec((1, 2, nc, LANES), lambda bi, p, i: (bi, p, 0, 0)),
                  pl.BlockSpec((1, 1, nc, V_TILE), lambda bi, p, i: (bi, p, 0, 0))],
        out_specs=pl.BlockSpec((1, tq, LANES), lambda bi, p, i: (bi, i, p)),
        out_shape=jax.ShapeDtypeStruct((b, n, HEAD_PAIRS * LANES), BF16),
        compiler_params=_params("parallel", "parallel", "arbitrary"),
        name="attn",
    )(q, k, v, kc, vc)


def _halo_specs(tm, n, width, halo):
    per = tm // halo
    last = n // halo - 1
    return (pl.BlockSpec((1, halo, width), lambda bi, i: (bi, jnp.maximum(i * per - 1, 0), 0)),
            pl.BlockSpec((1, tm, width), lambda bi, i: (bi, i, 0)),
            pl.BlockSpec((1, halo, width), lambda bi, i: (bi, jnp.minimum((i + 1) * per, last), 0)))


def _halo_masks():
    i = pl.program_id(1)
    has_prev = (i > 0).astype(F32)
    has_next = (i < pl.num_programs(1) - 1).astype(F32)
    return has_prev, has_next


CONV_ROWS = 128


def _mix_kernel(gp_ref, g_ref, gn_ref, a_ref, x_ref, g1_ref, dww_ref, dwb_ref, cg_ref, cb_ref,
                wo_ref, bo_ref, lg_ref, lb_ref, o_ref, win_ref, conv_ref):
    tm = g_ref.shape[1]
    has_prev, has_next = _halo_masks()
    for j in range(C_CONV // LANES):
        cols = slice(j * LANES, (j + 1) * LANES)
        win_ref[j, 0:HALO] = gp_ref[0, :, cols] * has_prev
        win_ref[j, HALO:HALO + tm] = g_ref[0, :, cols]
        win_ref[j, HALO + tm:] = gn_ref[0, :, cols] * has_next
    half = CONV_K // 2

    def rows(c, carry):
        base = pl.multiple_of(c * CONV_ROWS, CONV_ROWS)
        for j in range(C_CONV // LANES):
            cols = slice(j * LANES, (j + 1) * LANES)
            acc = jnp.zeros((CONV_ROWS, LANES), F32) + dwb_ref[:, cols]
            for kk in range(CONV_K):
                acc = acc + (win_ref[j, pl.ds(base + (HALO - half + kk), CONV_ROWS), :]
                             * dww_ref[kk:kk + 1, cols])
            conv_ref[pl.ds(base, CONV_ROWS), cols] = acc
        return carry

    lax.fori_loop(0, tm // CONV_ROWS, rows, 0)
    c = _silu(_norm(conv_ref[...]) * cg_ref[...] + cb_ref[...]).astype(BF16)
    y = _dot(c, wo_ref[:C_CONV, :]) + _dot(a_ref[0], wo_ref[C_CONV:, :]) + bo_ref[...]
    o_ref[0] = _norm(_norm(x_ref[0], ALPHA) + g1_ref[0] * y) * lg_ref[...] + lb_ref[...]


def _mix_call(glu, attn, x, g1, w, tm):
    b, n, d = x.shape
    tok = lambda bi, i: (bi, i, 0)
    row = lambda bi, i: (bi, 0, 0)
    return pl.pallas_call(
        _mix_kernel,
        grid=(b, n // tm),
        in_specs=[*_halo_specs(tm, n, C_CONV, HALO),
                  pl.BlockSpec((1, tm, HEAD_PAIRS * LANES), tok),
                  pl.BlockSpec((1, tm, d), tok),
                  pl.BlockSpec((1, 1, d), row),
                  _const_spec(w["dww"].shape), _const_spec(w["dwb"].shape),
                  _const_spec(w["cg"].shape), _const_spec(w["cb"].shape),
                  _const_spec(w["wo"].shape), _const_spec(w["bo"].shape),
                  _const_spec(w["l1g"].shape), _const_spec(w["l1b"].shape)],
        out_specs=pl.BlockSpec((1, tm, d), tok),
        out_shape=jax.ShapeDtypeStruct((b, n, d), F32),
        scratch_shapes=[pltpu.VMEM((C_CONV // LANES, tm + 2 * HALO, LANES), F32),
                        pltpu.VMEM((tm, C_CONV), F32)],
        compiler_params=_params("parallel", "parallel"),
        name="mix",
    )(glu, glu, glu, attn, x, g1, w["dww"], w["dwb"], w["cg"], w["cb"], w["wo"], w["bo"],
      w["l1g"], w["l1b"])


def _ffn_kernel(xp_ref, x_ref, xn_ref, sc_ref, sh_ref, g2_ref, wup_ref, fw_ref, fb_ref, wdn_ref,
                bdn_ref, lg_ref, lb_ref, o_ref, h_ref, u_ref, act_ref):
    tm = x_ref.shape[1]
    has_prev, has_next = _halo_masks()
    scale = 1.0 + sc_ref[0]
    shift = sh_ref[0]
    h_ref[...] = jnp.concatenate([(xp_ref[0] * scale + shift) * has_prev,
                                  x_ref[0] * scale + shift,
                                  (xn_ref[0] * scale + shift) * has_next], axis=0).astype(BF16)

    groups = FF_CHUNK // LANES

    def conv(slot, col0, u):
        outs = []
        for j in range(groups):
            u_ref[slot + j] = u[:, j * LANES:(j + 1) * LANES]
            cols = slice(col0 + j * LANES, col0 + (j + 1) * LANES)
            acc = fb_ref[:, cols]
            for kk in range(FFN_K):
                acc = acc + u_ref[slot + j, pl.ds(FFN_HALO - 1 + kk, tm), :] * fw_ref[kk:kk + 1, cols]
            outs.append(acc)
        return outs

    for c in range(D_FF // FF_CHUNK):
        g = conv(4 * (c % 2), c * FF_CHUNK, _dot(h_ref[...], wup_ref[:, c * FF_CHUNK:(c + 1) * FF_CHUNK]))
        v = conv(4 * (c % 2) + 2, D_FF + c * FF_CHUNK,
                 _dot(h_ref[...], wup_ref[:, D_FF + c * FF_CHUNK:D_FF + (c + 1) * FF_CHUNK]))
        for j in range(groups):
            cols = slice(c * FF_CHUNK + j * LANES, c * FF_CHUNK + (j + 1) * LANES)
            act_ref[:, cols] = (_silu(g[j]) * v[j]).astype(BF16)

    for r0 in range(0, tm, tm // 2):
        rows = slice(r0, r0 + tm // 2)
        f = _dot(act_ref[rows, :], wdn_ref[...]) + bdn_ref[...]
        o_ref[0, rows] = _norm(ALPHA * x_ref[0, rows] + g2_ref[0] * f) * lg_ref[...] + lb_ref[...]


def _ffn_call(xm, sc, sh, g2, w, tm):
    b, n, d = xm.shape
    tok = lambda bi, i: (bi, i, 0)
    row = lambda bi, i: (bi, 0, 0)
    return pl.pallas_call(
        _ffn_kernel,
        grid=(b, n // tm),
        in_specs=[*_halo_specs(tm, n, d, FFN_HALO),
                  pl.BlockSpec((1, 1, d), row), pl.BlockSpec((1, 1, d), row), pl.BlockSpec((1, 1, d), row),
                  _const_spec(w["wup"].shape), _const_spec(w["fw"].shape), _const_spec(w["fb"].shape),
                  _const_spec(w["wdn"].shape), _const_spec(w["bdn"].shape),
                  _const_spec(w["l2g"].shape), _const_spec(w["l2b"].shape)],
        out_specs=pl.BlockSpec((1, tm, d), tok),
        out_shape=jax.ShapeDtypeStruct((b, n, d), F32),
        scratch_shapes=[pltpu.VMEM((tm + 2 * FFN_HALO, d), BF16),
                        pltpu.VMEM((4 * FF_CHUNK // LANES, tm + 2 * FFN_HALO, LANES), F32),
                        pltpu.VMEM((tm, D_FF), BF16)],
        compiler_params=_params("parallel", "parallel"),
        name="ffn",
    )(xm, xm, xm, sc, sh, g2, w["wup"], w["fw"], w["fb"], w["wdn"], w["bdn"], w["l2g"], w["l2b"])


def _rope_tables(n):
    t = jnp.arange(n)
    n_freq = QK_ROPE // 4
    inv = ROPE_THETA ** (-jnp.arange(n_freq, dtype=F32) / n_freq)
    ang = jnp.concatenate([(t // GRID_W)[:, None] * inv, (t % GRID_W)[:, None] * inv], axis=-1)
    cos, sin = jnp.cos(ang), jnp.sin(ang)
    pad = jnp.zeros((n, LANES - QK_NOPE - QK_ROPE), F32)
    cos_t = jnp.concatenate([jnp.ones((n, QK_NOPE), F32), cos, cos, pad], axis=-1)
    sin_t = jnp.concatenate([jnp.zeros((n, QK_NOPE), F32), sin, sin, pad], axis=-1)
    return cos_t, sin_t


def _partner(w_rope):
    half = w_rope.shape[-1] // 2
    return jnp.concatenate([-w_rope[..., half:], w_rope[..., :half]], axis=-1)


def _layout_weights(w_in, w_uq, w_ukv):
    d = w_in.shape[0]
    tail = LANES - QK_NOPE - QK_ROPE
    w_in = w_in.astype(BF16)
    z = lambda r, c: jnp.zeros((r, c), BF16)
    kr = w_in[:, _Z_KR:_Z_KR + QK_ROPE]
    win = jnp.concatenate([w_in[:, :_Z_KR], z(d, QK_NOPE), kr, _partner(kr)], axis=-1)
    win_ctx = jnp.concatenate([w_in[:, _Z_CKV:_Z_KR], z(d, QK_NOPE), kr, z(d, tail)], axis=-1)

    wq = w_uq.reshape(Q_LORA, N_HEADS, QK_NOPE + QK_ROPE)
    wq = jnp.concatenate([wq, _partner(wq[..., QK_NOPE:])], axis=-1).reshape(Q_LORA, N_HEADS * LANES)
    wkv = w_ukv.reshape(KV_LORA, N_HEADS, QK_NOPE + V_DIM)
    wk = wkv[..., :QK_NOPE].reshape(KV_LORA, N_HEADS * QK_NOPE)
    wv = wkv[..., QK_NOPE:].reshape(KV_LORA, N_HEADS * V_DIM)
    cast = lambda a: a.astype(BF16)
    return dict(win=cast(win), win_ctx=cast(win_ctx), wq=cast(wq), wk=cast(wk), wv=cast(wv))


def kernel(x, c, ctx, c_ctx, w_ada, b_ada, w_in, conv_dw_w, conv_dw_b, conv_ln_g, conv_ln_b, q_norm_g,
           w_uq, kv_norm_g, w_ukv, w_o, b_o, ln1_g, ln1_b, w_up, ffn_dw_w, ffn_dw_b, w_down, b_down,
           ln2_g, ln2_b):
    b, n, d = x.shape
    tm = 512
    tq = 4 * ATTN_ROWS
    assert DEPTH == 1 and w_ada.shape[0] == 1
    assert n % tm == 0 and n % tq == 0 and tm % HALO == 0

    c_rows = jnp.concatenate([c, c_ctx[None, :], jnp.zeros((16 - b - 1, d), F32)], axis=0)
    mod = _mod_call(c_rows, w_ada[0], b_ada).reshape(16, 6, 1, d)
    sh1, sc1, g1, sh2, sc2, g2 = (mod[:b, j] for j in range(6))
    sh1c, sc1c = mod[b:b + 1, 0], mod[b:b + 1, 1]

    row2 = lambda a: a.reshape(1, -1)
    w = _layout_weights(w_in[0], w_uq[0], w_ukv[0])
    w.update(gq=row2(q_norm_g[0]), gkv=row2(kv_norm_g[0]))
    cos_t, sin_t = _rope_tables(n)

    glu, q, k_lat, v_lat = _proj_call(x, sc1, sh1, cos_t, sin_t, w, 2 * PROJ_ROWS)
    k_ctx, v_ctx = _proj_ctx_call(ctx, sc1c, sh1c, w)
    attn = _attn_call(q, k_lat, v_lat, k_ctx, v_ctx, tq)

    wm = dict(dww=conv_dw_w[0], dwb=row2(conv_dw_b[0]), cg=row2(conv_ln_g[0]), cb=row2(conv_ln_b[0]),
              wo=w_o[0].astype(BF16), bo=row2(b_o[0]), l1g=row2(ln1_g[0]), l1b=row2(ln1_b[0]))
    x_mid = _mix_call(glu, attn, x, g1, wm, 2 * tm)

    wf = dict(wup=w_up[0].astype(BF16), fw=ffn_dw_w[0], fb=row2(ffn_dw_b[0]), wdn=w_down[0].astype(BF16),
              bdn=row2(b_down[0]), l2g=row2(ln2_g[0]), l2b=row2(ln2_b[0]))
    return _ffn_call(x_mid, sc2, sh2, g2, wf, tm)
```
